```python
import functools
import jax, jax.numpy as jnp
from jax import lax
import numpy as np

D_MODEL = 1024
BATCH = 4
SEQ = 4096
DEPTH = 2

GRID_W = 64
CTX_LEN = 256
MLSTM_HEADS = 8
MLSTM_DQK = 64
MLSTM_DV = 128
MLSTM_CHUNK = 64
GATE_SOFT_CAP = 15.0
RWKV_HEADS = 16
RWKV_N = 64
RWKV_DIM = RWKV_HEADS * RWKV_N
DECAY_LORA = 64
AAA_LORA = 64
GATE_LORA = 128
GN_EPS = 64e-5
D_FF = 2816
N_EXPERTS = 8
TOP_K = 2
D_FF_EXPERT = 3584
NORM_EPS = 1e-6

MLSTM_QK_W = MLSTM_HEADS * MLSTM_DQK
MLSTM_V_W = MLSTM_HEADS * MLSTM_DV
RWKV_IN_W = 3 * RWKV_DIM + 2 * DECAY_LORA + 2 * AAA_LORA + GATE_LORA
IN_SIZES = (MLSTM_QK_W, MLSTM_QK_W, MLSTM_V_W, MLSTM_V_W, 4 * MLSTM_HEADS, RWKV_IN_W, 2 * D_MODEL)
D_IN = sum(IN_SIZES)
RWKV_SIZES = (RWKV_DIM, RWKV_DIM, RWKV_DIM, DECAY_LORA, DECAY_LORA, AAA_LORA, AAA_LORA, GATE_LORA)
N_DENSE = (DEPTH + 1) // 2
N_MOE = DEPTH // 2

kernel_name = 'hybrid_mlstm_rwkv7_moe_prefix_dit'


def _split(a, sizes):
    idx = np.cumsum(sizes)[:-1].tolist()
    return jnp.split(a, idx, axis=-1)


def rmsnorm(x, g):
    xf = x.astype(jnp.float32)
    y = xf * lax.rsqrt(jnp.mean(xf * xf, axis=-1, keepdims=True) + NORM_EPS)
    return (y * g.astype(jnp.float32)).astype(x.dtype)


def modulate(h, shift, scale):
    return h * (1 + scale) + shift


def soft_cap(a):
    return GATE_SOFT_CAP * jnp.tanh(a / GATE_SOFT_CAP)


def heads(a, n_heads):
    B, T, _ = a.shape
    return a.reshape(B, T, n_heads, -1).transpose(0, 2, 1, 3)


def grid_shift(x):
    B, T, C = x.shape
    rows = T // GRID_W
    g = x.reshape(B, rows, GRID_W, C)
    q = C // 4
    left = jnp.pad(g[:, :, :-1, :q], ((0, 0), (0, 0), (1, 0), (0, 0)))
    right = jnp.pad(g[:, :, 1:, q:2 * q], ((0, 0), (0, 0), (0, 1), (0, 0)))
    up = jnp.pad(g[:, :-1, :, 2 * q:3 * q], ((0, 0), (1, 0), (0, 0), (0, 0)))
    down = jnp.pad(g[:, 1:, :, 3 * q:], ((0, 0), (0, 1), (0, 0), (0, 0)))
    return jnp.concatenate([left, right, up, down], axis=-1).reshape(B, T, C)


def seq_shift(x):
    h = x.shape[-1] // 2
    prev = jnp.pad(x[:, :-1, :h], ((0, 0), (1, 0), (0, 0)))
    nxt = jnp.pad(x[:, 1:, h:], ((0, 0), (0, 1), (0, 0)))
    return jnp.concatenate([prev, nxt], axis=-1)


def mlstm_chunkwise(q, k, v, i_pre, f_pre, state):
    B, H, T, dk = q.shape
    L = MLSTM_CHUNK
    nc = T // L
    q = q * (dk ** -0.5)
    logf = jax.nn.log_sigmoid(f_pre)
    causal = jnp.tril(jnp.ones((L, L), dtype=bool))

    def to_chunks(a):
        return jnp.moveaxis(a.reshape(B, H, nc, L, *a.shape[3:]), 2, 0)

    def step(carry, inp):
        C, n, m = carry
        qc, kc, vc, ic, lf = inp
        b = jnp.cumsum(lf, axis=-1)
        d = jnp.where(causal, b[..., :, None] - b[..., None, :] + ic[..., None, :], -jnp.inf)
        inter = b + m[..., None]
        m_t = jnp.maximum(inter, jnp.max(d, axis=-1))
        w_intra = jnp.exp(d - m_t[..., None])
        w_inter = jnp.exp(inter - m_t)
        s = jnp.einsum('bhtd,bhsd->bhts', qc, kc) * w_intra
        num = jnp.einsum('bhts,bhsv->bhtv', s, vc) + w_inter[..., None] * jnp.einsum('bhvd,bhtd->bhtv', C, qc)
        den = jnp.sum(s, axis=-1) + w_inter * jnp.einsum('bhd,bhtd->bht', n, qc)
        h = num / jnp.maximum(jnp.abs(den), jnp.exp(-m_t))[..., None]
        b_last = b[..., -1]
        w_src = b_last[..., None] - b + ic
        m_new = jnp.maximum(b_last + m, jnp.max(w_src, axis=-1))
        a_src = jnp.exp(w_src - m_new[..., None])
        a_old = jnp.exp(b_last + m - m_new)
        C = a_old[..., None, None] * C + jnp.einsum('bhs,bhsv,bhsd->bhvd', a_src, vc, kc)
        n = a_old[..., None] * n + jnp.einsum('bhs,bhsd->bhd', a_src, kc)
        return (C, n, m_new), h

    xs = (to_chunks(q), to_chunks(k), to_chunks(v), to_chunks(i_pre), to_chunks(logf))
    state, h = lax.scan(step, state, xs)
    h = jnp.moveaxis(h, 0, 2).reshape(B, H, T, -1)
    return h, state


def mlstm_mixer(q, k, v, gates, qc, kc, vc, gatesc, b_gate):
    def prep(q, k, v, gates):
        B, T, _ = q.shape
        pre = soft_cap(gates.astype(jnp.float32).reshape(B, T, 4, MLSTM_HEADS) + b_gate).transpose(2, 0, 3, 1)
        f32 = lambda a: a.astype(jnp.float32)
        return heads(f32(q), MLSTM_HEADS), heads(f32(k), MLSTM_HEADS), heads(f32(v), MLSTM_HEADS), pre

    ql, kl, vl, pl = prep(q, k, v, gates)
    qx, kx, vx, px = prep(qc, kc, vc, gatesc)
    B = q.shape[0]
    zero = (jnp.zeros((B, MLSTM_HEADS, MLSTM_DV, MLSTM_DQK), jnp.float32),
            jnp.zeros((B, MLSTM_HEADS, MLSTM_DQK), jnp.float32),
            jnp.zeros((B, MLSTM_HEADS), jnp.float32))
    fl = lambda a: jnp.flip(a, axis=2)
    hc_f, st_f = mlstm_chunkwise(qx, kx, vx, px[0], px[1], zero)
    h_f, _ = mlstm_chunkwise(ql, kl, vl, pl[0], pl[1], st_f)
    hc_b, st_b = mlstm_chunkwise(fl(qx), fl(kx), fl(vx), fl(px[2]), fl(px[3]), zero)
    h_b, _ = mlstm_chunkwise(fl(ql), fl(kl), fl(vl), fl(pl[2]), fl(pl[3]), st_b)
    return h_f + fl(h_b), hc_f + fl(hc_b)


def mlstm_out(h, o, g_norm):
    B, H, T, DV = h.shape
    h = h.transpose(0, 2, 1, 3)
    h = h * lax.rsqrt(jnp.mean(h * h, axis=-1, keepdims=True) + NORM_EPS)
    h = h.reshape(B, T, H * DV) * g_norm
    return (h * jax.nn.sigmoid(o.astype(jnp.float32))).astype(o.dtype)


def rwkv7_prepare(p, p_shift, mu, w0, w2, a0, a2, g2, k_k, k_a):
    B, T, _ = p.shape
    pf = p.astype(jnp.float32)
    xm = pf + (p_shift.astype(jnp.float32) - pf) * mu
    r, k, v, wd_f, wd_b, ad_f, ad_b, gd = _split(xm, RWKV_SIZES)
    hd = lambda a: a.reshape(B, T, RWKV_HEADS, RWKV_N)
    kk = hd(k * k_k)
    kk = kk / jnp.maximum(jnp.sqrt(jnp.sum(kk * kk, axis=-1, keepdims=True)), 1e-12)
    dirs = []
    for d, (wd, ad) in enumerate(((wd_f, ad_f), (wd_b, ad_b))):
        w = -jax.nn.softplus(-(w0[d] + jnp.tanh(wd) @ w2[d])) - 0.5
        decay = jnp.exp(-jnp.exp(w))
        a = jax.nn.sigmoid(a0[d] + ad @ a2[d])
        kd = k * (1 + (a - 1) * k_a)
        dirs.append((hd(decay), hd(a), hd(kd)))
    g = jax.nn.sigmoid(gd) @ g2
    return hd(r), hd(v), kk, g, dirs


def rwkv7_scan(r, w, k, v, kk, a, state, reverse):
    def step(S, inp):
        r_t, w_t, k_t, v_t, kk_t, a_t = inp
        sa = jnp.einsum('bhvk,bhk->bhv', S, kk_t)
        S = (S * w_t[:, :, None, :] - sa[..., None] * (kk_t * a_t)[:, :, None, :]
             + v_t[..., None] * k_t[:, :, None, :])
        return S, jnp.einsum('bhvk,bhk->bhv', S, r_t)
    xs = tuple(jnp.moveaxis(t, 1, 0) for t in (r, w, k, v, kk, a))
    state, y = lax.scan(step, state, xs, reverse=reverse)
    return jnp.moveaxis(y, 0, 1), state


def rwkv7_mixer(p, pc, need_ctx, mu, w0, w2, a0, a2, g2, k_k, k_a, r_k, ln_w, ln_b):
    prm = (mu, w0, w2, a0, a2, g2, k_k, k_a)
    lat = rwkv7_prepare(p, grid_shift(p), *prm)
    ctxp = rwkv7_prepare(pc, seq_shift(pc), *prm)
    zero = jnp.zeros((p.shape[0], RWKV_HEADS, RWKV_N, RWKV_N), jnp.float32)

    def run(prep, s_f, s_b):
        r, v, kk, g, ((dec_f, a_f, k_f), (dec_b, a_b, k_b)) = prep
        y_f, s_f = rwkv7_scan(r, dec_f, k_f, v, kk, a_f, s_f, False)
        y_b, s_b = rwkv7_scan(r, dec_b, k_b, v, kk, a_b, s_b, True)
        return y_f + y_b, s_f, s_b

    def finish(y, prep):
        r, v, kk, g, dirs = prep
        B, T = y.shape[:2]
        mean = jnp.mean(y, axis=-1, keepdims=True)
        var = jnp.mean(jnp.square(y - mean), axis=-1, keepdims=True)
        yn = ((y - mean) * lax.rsqrt(var + GN_EPS)).reshape(B, T, -1) * ln_w + ln_b
        bonus = sum(jnp.sum(r * kd * r_k, axis=-1, keepdims=True) * v for (_, _, kd) in dirs)
        return (yn + bonus.reshape(B, T, -1)) * g

    yc, s_f, s_b = run(ctxp, zero, zero)
    y, _, _ = run(lat, s_f, s_b)
    out = finish(y, lat).astype(p.dtype)
    out_c = finish(yc, ctxp).astype(pc.dtype) if need_ctx else None
    return out, out_c


def hybrid_mixer(h, hc, need_ctx, w_in, b_gate, g_mnorm, mu, w0, w2, a0, a2, g2, k_k, k_a, r_k,
                 ln_w, ln_b, w_pm, w_pr, w_out):
    q, k, v, o, gates, prw, mg = _split(h @ w_in, IN_SIZES)
    qc, kc, vc, oc, gatesc, prwc, mgc = _split(hc @ w_in, IN_SIZES)
    hm, hmc = mlstm_mixer(q, k, v, gates, qc, kc, vc, gatesc, b_gate)
    yr, yrc = rwkv7_mixer(prw, prwc, need_ctx, mu, w0, w2, a0, a2, g2, k_k, k_a, r_k, ln_w, ln_b)

    def merge(hm, o, yr, mg):
        ym = mlstm_out(hm, o, g_mnorm) @ w_pm
        yr = yr @ w_pr
        g_m, g_r = jnp.split(jax.nn.sigmoid(mg), 2, axis=-1)
        return (g_m * ym + g_r * yr) @ w_out

    y = merge(hm, o, yr, mg)
    yc = merge(hmc, oc, yrc, mgc) if need_ctx else None
    return y, yc


def swiglu(h, wg, wu, wd):
    return (jax.nn.silu(h @ wg) * (h @ wu)) @ wd


def moe_swiglu(h, w_router, wg, wu, wd):
    logits = (h @ w_router).astype(jnp.float32)
    top_v, top_i = lax.top_k(logits, TOP_K)
    probs = jax.nn.softmax(top_v, axis=-1)
    gates = jnp.einsum('btk,btke->bte', probs, jax.nn.one_hot(top_i, N_EXPERTS, dtype=jnp.float32)).astype(h.dtype)
    out = jnp.zeros_like(h)
    for e in range(N_EXPERTS):
        out = out + gates[..., e:e + 1] * swiglu(h, wg[e], wu[e], wd[e])
    return out


def setup_inputs(seed: int = 0) -> dict:
    key = jax.random.key(seed)
    ks = iter(jax.random.split(key, 40))
    nrm = lambda shape, s: s * jax.random.normal(next(ks), shape, jnp.float32)
    D, Dr, Dv = D_MODEL, RWKV_DIM, MLSTM_V_W
    gate_base = jnp.array([-2.0, 3.0, -2.0, 3.0], jnp.float32)[None, :, None]
    return {
        'x': nrm((BATCH, SEQ, D), 1.0),
        'c': nrm((BATCH, D), 1.0),
        'ctx': nrm((BATCH, CTX_LEN, D), 1.0),
        'c_ctx': nrm((D,), 1.0),
        'w_ada': nrm((DEPTH, D, 6 * D), 0.5 * D ** -0.5),
        'b_ada': nrm((DEPTH, 6 * D), 0.01),
        'g_norm_mix': 1.0 + nrm((DEPTH, D), 0.02),
        'g_norm_ffn': 1.0 + nrm((DEPTH, D), 0.02),
        'w_in': nrm((DEPTH, D, D_IN), D ** -0.5),
        'b_mlstm_gate': gate_base + nrm((DEPTH, 4, MLSTM_HEADS), 0.5),
        'g_mlstm_norm': 1.0 + nrm((DEPTH, Dv), 0.02),
        'mu_shift': jax.random.uniform(next(ks), (DEPTH, RWKV_IN_W), jnp.float32),
        'w0': jax.random.uniform(next(ks), (DEPTH, 2, Dr), jnp.float32, -5.5, -0.5),
        'w2': nrm((DEPTH, 2, DECAY_LORA, Dr), 0.1 * DECAY_LORA ** -0.5),
        'a0': nrm((DEPTH, 2, Dr), 0.3),
        'a2': nrm((DEPTH, 2, AAA_LORA, Dr), 0.5 * AAA_LORA ** -0.5),
        'g2': nrm((DEPTH, GATE_LORA, Dr), GATE_LORA ** -0.5),
        'k_k': 0.85 + nrm((DEPTH, Dr), 0.1),
        'k_a': 1.0 + nrm((DEPTH, Dr), 0.1),
        'r_k': nrm((DEPTH, RWKV_HEADS, RWKV_N), 0.1),
        'ln_w': 1.0 + nrm((DEPTH, Dr), 0.02),
        'ln_b': nrm((DEPTH, Dr), 0.01),
        'w_proj_mlstm': nrm((DEPTH, Dv, D), Dv ** -0.5),
        'w_proj_rwkv': nrm((DEPTH, Dr, D), Dr ** -0.5),
        'w_out': nrm((DEPTH, D, D), D ** -0.5),
        'w_ff_gate': nrm((N_DENSE, D, D_FF), D ** -0.5),
        'w_ff_up': nrm((N_DENSE, D, D_FF), D ** -0.5),
        'w_ff_down': nrm((N_DENSE, D_FF, D), D_FF ** -0.5),
        'w_router': nrm((N_MOE, D, N_EXPERTS), D ** -0.5),
        'w_exp_gate': nrm((N_MOE, N_EXPERTS, D, D_FF_EXPERT), D ** -0.5),
        'w_exp_up': nrm((N_MOE, N_EXPERTS, D, D_FF_EXPERT), D ** -0.5),
        'w_exp_down': nrm((N_MOE, N_EXPERTS, D_FF_EXPERT, D), D_FF_EXPERT ** -0.5),
        'g_final': 1.0 + nrm((D,), 0.02),
    }


def reference(x, c, ctx, c_ctx, w_ada, b_ada, g_norm_mix, g_norm_ffn, w_in, b_mlstm_gate, g_mlstm_norm,
              mu_shift, w0, w2, a0, a2, g2, k_k, k_a, r_k, ln_w, ln_b, w_proj_mlstm, w_proj_rwkv, w_out,
              w_ff_gate, w_ff_up, w_ff_down, w_router, w_exp_gate, w_exp_up, w_exp_down, g_final):
    xc = ctx
    s_lat = jax.nn.silu(c)
    s_ctx = jax.nn.silu(c_ctx)
    for l in range(DEPTH):
        need_ctx = l < DEPTH - 1
        mod = (s_lat @ w_ada[l] + b_ada[l])[:, None, :]
        mod_c = s_ctx @ w_ada[l] + b_ada[l]
        sh1, sc1, gt1, sh2, sc2, gt2 = jnp.split(mod, 6, axis=-1)
        sh1c, sc1c, gt1c, sh2c, sc2c, gt2c = jnp.split(mod_c, 6, axis=-1)
        h = modulate(rmsnorm(x, g_norm_mix[l]), sh1, sc1)
        hc = modulate(rmsnorm(xc, g_norm_mix[l]), sh1c, sc1c)
        y, yc = hybrid_mixer(h, hc, need_ctx, w_in[l], b_mlstm_gate[l], g_mlstm_norm[l], mu_shift[l],
                             w0[l], w2[l], a0[l], a2[l], g2[l], k_k[l], k_a[l], r_k[l], ln_w[l], ln_b[l],
                             w_proj_mlstm[l], w_proj_rwkv[l], w_out[l])
        x = x + gt1 * y
        i = l // 2
        if l % 2 == 0:
            ffn = functools.partial(swiglu, wg=w_ff_gate[i], wu=w_ff_up[i], wd=w_ff_down[i])
        else:
            ffn = functools.partial(moe_swiglu, w_router=w_router[i], wg=w_exp_gate[i], wu=w_exp_up[i],
                                    wd=w_exp_down[i])
        x = x + gt2 * ffn(modulate(rmsnorm(x, g_norm_ffn[l]), sh2, sc2))
        if need_ctx:
            xc = xc + gt1c * yc
            xc = xc + gt2c * ffn(modulate(rmsnorm(xc, g_norm_ffn[l]), sh2c, sc2c))
    return rmsnorm(x, g_final)
```

```python
import functools

import jax
import jax.numpy as jnp
import numpy as np
from jax import lax
from jax.experimental import pallas as pl
from jax.experimental.pallas import tpu as pltpu

F32 = jnp.float32
BF16 = jnp.bfloat16
HIGHEST = lax.Precision.HIGHEST

GRID_W = 64
MLSTM_HEADS = 8
MLSTM_DQK = 64
MLSTM_DV = 128
GATE_SOFT_CAP = 15.0
RWKV_HEADS = 16
RWKV_N = 64
DECAY_LORA = 64
AAA_LORA = 64
GATE_LORA = 128
GN_EPS = 64e-5
N_EXPERTS = 8
NORM_EPS = 1e-6

LANES = 128
MXU_DIM = 256
ROW_TILE = 256
VMEM_LIMIT = 56 * 1024 * 1024

SH1, SC1, GT1, SH2, SC2, GT2 = range(6)
MOD_ROWS = 8


def _cparams(sem):
    return pltpu.CompilerParams(dimension_semantics=sem, vmem_limit_bytes=VMEM_LIMIT)


def _sigmoid(x):
    return 1.0 / (1.0 + jnp.exp(-x))


def _block_diag_ones(block):
    r = lax.broadcasted_iota(jnp.int32, (MXU_DIM, MXU_DIM), 0)
    c = lax.broadcasted_iota(jnp.int32, (MXU_DIM, MXU_DIM), 1)
    sh = int(np.log2(block))
    return jnp.where((r >> sh) == (c >> sh), 1.0, 0.0).astype(BF16)


def _group_sum(x, bd):
    outs = []
    for c in range(x.shape[1] // MXU_DIM):
        xs = x[:, c * MXU_DIM:(c + 1) * MXU_DIM]
        hi = xs.astype(BF16)
        r1 = xs - hi.astype(F32)
        mid = r1.astype(BF16)
        lo = (r1 - mid.astype(F32)).astype(BF16)
        acc = jnp.dot(hi, bd, preferred_element_type=F32)
        acc += jnp.dot(mid, bd, preferred_element_type=F32)
        acc += jnp.dot(lo, bd, preferred_element_type=F32)
        outs.append(acc)
    return jnp.concatenate(outs, axis=1)


def _rms_mod(x, g, mod, shift_row, scale_row):
    ms = jnp.mean(x * x, axis=-1, keepdims=True)
    y = x * lax.rsqrt(ms + NORM_EPS) * g
    return y * (1.0 + mod[scale_row:scale_row + 1, :]) + mod[shift_row:shift_row + 1, :]


def _ada_kernel(s_ref, w_ref, b_ref, o_ref):
    s = s_ref[...]
    s = s * _sigmoid(s)
    o_ref[...] = jnp.dot(s, w_ref[...], precision=HIGHEST, preferred_element_type=F32) + b_ref[...]


def ada_mod(cond, w_ada, b_ada):
    depth, d, n = w_ada.shape
    tn = n // 4
    return pl.pallas_call(
        _ada_kernel,
        grid=(depth, n // tn),
        in_specs=[pl.BlockSpec((MOD_ROWS, d), lambda l, j: (0, 0)),
                  pl.BlockSpec((None, d, tn), lambda l, j: (l, 0, j)),
                  pl.BlockSpec((None, 1, tn), lambda l, j: (l, 0, j))],
        out_specs=pl.BlockSpec((None, MOD_ROWS, tn), lambda l, j: (l, 0, j)),
        out_shape=jax.ShapeDtypeStruct((depth, MOD_ROWS, n), F32),
        compiler_params=_cparams(("arbitrary", "arbitrary")),
        name="ada_mod",
    )(cond, w_ada, b_ada.reshape(depth, 1, n))


def _norm_mod_kernel(x_ref, g_ref, mod_ref, hb_ref, hf_ref):
    h = _rms_mod(x_ref[...], g_ref[...], mod_ref[...], SH1, SC1)
    hb_ref[...] = h.astype(BF16)
    hf_ref[...] = h


def norm_mod(x_all, g, modtab):
    b, tt, d = x_all.shape
    nt = tt // ROW_TILE
    ctx_tile = nt - 1
    spec = pl.BlockSpec((None, ROW_TILE, d), lambda bi, i: (bi, i, 0))
    return pl.pallas_call(
        _norm_mod_kernel,
        grid=(b, nt),
        in_specs=[spec,
                  pl.BlockSpec((1, d), lambda bi, i: (0, 0)),
                  pl.BlockSpec((None, None, MOD_ROWS, d), lambda bi, i: (bi, jnp.where(i == ctx_tile, 0, 1), 0, 0))],
        out_specs=[spec, spec],
        out_shape=[jax.ShapeDtypeStruct((b, tt, d), BF16), jax.ShapeDtypeStruct((b, tt, d), F32)],
        compiler_params=_cparams(("parallel", "parallel")),
        name="norm_mod",
    )(x_all, g.reshape(1, d), modtab)


def _mm_kernel(a_ref, w_ref, o_ref, *, precision):
    o_ref[...] = jnp.dot(a_ref[...], w_ref[...], precision=precision,
                         preferred_element_type=F32).astype(o_ref.dtype)


def matmul(a, w, tm, tn, out_dtype=F32, precision=None):
    m, k = a.shape
    n = w.shape[1]
    return pl.pallas_call(
        functools.partial(_mm_kernel, precision=precision),
        grid=(n // tn, m // tm),
        in_specs=[pl.BlockSpec((tm, k), lambda j, i: (i, 0)),
                  pl.BlockSpec((k, tn), lambda j, i: (0, j))],
        out_specs=pl.BlockSpec((tm, tn), lambda j, i: (i, j)),
        out_shape=jax.ShapeDtypeStruct((m, n), out_dtype),
        compiler_params=_cparams(("parallel", "parallel")),
        name="matmul",
    )(a, w)


def _mlstm_kernel(q_ref, k_ref, v_ref, gi_ref, gf_ref, bi_ref, bf_ref, o_ref, ct_ref, m_ref, *, rev):
    L = ROW_TILE
    H, DQK, DV = MLSTM_HEADS, MLSTM_DQK, MLSTM_DV

    @pl.when(pl.program_id(1) == 0)
    def _():
        ct_ref[...] = jnp.zeros_like(ct_ref)
        m_ref[...] = jnp.zeros_like(m_ref)

    def cap(a):
        return GATE_SOFT_CAP * jnp.tanh(a / GATE_SOFT_CAP)

    ic_all = cap(gi_ref[...] + bi_ref[...])
    fp = cap(gf_ref[...] + bf_ref[...])
    lf_all = jnp.minimum(fp, 0.0) - jnp.log(1.0 + jnp.exp(-jnp.abs(fp)))
    row = lax.broadcasted_iota(jnp.int32, (L, L), 0)
    col = lax.broadcasted_iota(jnp.int32, (L, L), 1)
    tri = (col >= row) if rev else (col <= row)
    b_all = jnp.dot(jnp.where(tri, 1.0, 0.0), lf_all, precision=HIGHEST, preferred_element_type=F32)
    src_all = ic_all - b_all
    src_t = src_all.T
    last = 0 if rev else L - 1
    lane0 = lax.broadcasted_iota(jnp.int32, (L, LANES), 1) == 0
    ones_blk = jnp.where(lane0, 1.0, 0.0).astype(BF16)
    scale = DQK ** -0.5

    for h in range(H):
        ln = (H if rev else 0) + h
        bcol = b_all[:, ln:ln + 1]
        iccol = ic_all[:, ln:ln + 1]
        srow = src_t[ln:ln + 1, :]
        m_prev = m_ref[h][0:1, 0:1]
        dmat = jnp.where(tri, bcol + srow, -jnp.inf)
        inter = bcol + m_prev
        m_t = jnp.maximum(inter, jnp.max(dmat, axis=-1, keepdims=True))
        w_intra = jnp.exp(dmat - m_t)
        w_inter = jnp.exp(inter - m_t)
        qb = (q_ref[:, h * DQK:(h + 1) * DQK] * scale).astype(BF16)
        kf = k_ref[:, h * DQK:(h + 1) * DQK]
        kb = kf.astype(BF16)
        vaug = jnp.concatenate([v_ref[:, h * DV:(h + 1) * DV].astype(BF16), ones_blk], axis=1)
        sqk = lax.dot_general(qb, kb, (((1,), (1,)), ((), ())), preferred_element_type=F32)
        sw = (sqk * w_intra).astype(BF16)
        ct = ct_ref[h]
        num_aug = (jnp.dot(sw, vaug, preferred_element_type=F32)
                   + w_inter * jnp.dot(qb, ct.astype(BF16), preferred_element_type=F32))
        num = num_aug[:, :DV]
        den = num_aug[:, DV:DV + 1]
        o_ref[:, h * DV:(h + 1) * DV] = num / jnp.maximum(jnp.abs(den), jnp.exp(-m_t))
        b_last = bcol[last:last + 1, :]
        w_src = b_last - bcol + iccol
        m_new = jnp.maximum(b_last + m_prev, jnp.max(w_src, axis=0, keepdims=True))
        a_src = jnp.exp(w_src - m_new)
        a_old = jnp.exp(b_last + m_prev - m_new)
        ks = (kf * a_src).astype(BF16)
        upd = lax.dot_general(ks, vaug, (((0,), (0,)), ((), ())), preferred_element_type=F32)
        ct_ref[h] = a_old * ct + upd
        m_ref[h] = jnp.broadcast_to(m_new, m_ref.shape[1:])


def mlstm_scan(qkvo, gates, bias_i, bias_f, rev):
    b, tt, _ = qkvo.shape
    nt = tt // ROW_TILE
    ctx_tile = nt - 1
    hq = MLSTM_HEADS * MLSTM_DQK
    hv = MLSTM_HEADS * MLSTM_DV
    if rev:
        order = lambda s: jnp.where(s == 0, ctx_tile, ctx_tile - s)
    else:
        order = lambda s: jnp.where(s == 0, ctx_tile, s - 1)
    return pl.pallas_call(
        functools.partial(_mlstm_kernel, rev=rev),
        grid=(b, nt),
        in_specs=[pl.BlockSpec((None, ROW_TILE, hq), lambda bi, s: (bi, order(s), 0)),
                  pl.BlockSpec((None, ROW_TILE, hq), lambda bi, s: (bi, order(s), 1)),
                  pl.BlockSpec((None, ROW_TILE, hv), lambda bi, s: (bi, order(s), 1)),
                  pl.BlockSpec((None, ROW_TILE, LANES), lambda bi, s: (bi, order(s), 0)),
                  pl.BlockSpec((None, ROW_TILE, LANES), lambda bi, s: (bi, order(s), 1)),
                  pl.BlockSpec((1, LANES), lambda bi, s: (0, 0)),
                  pl.BlockSpec((1, LANES), lambda bi, s: (0, 0))],
        out_specs=pl.BlockSpec((None, ROW_TILE, hv), lambda bi, s: (bi, order(s), 0)),
        out_shape=jax.ShapeDtypeStruct((b, tt, hv), F32),
        scratch_shapes=[pltpu.VMEM((MLSTM_HEADS, MLSTM_DQK, 2 * MLSTM_DV), F32),
                        pltpu.VMEM((MLSTM_HEADS, 8, LANES), F32)],
        compiler_params=_cparams(("parallel", "arbitrary")),
        name="mlstm_rev" if rev else "mlstm_fwd",
    )(qkvo, qkvo, qkvo, gates, gates, bias_i, bias_f)


def _rwkv_prep_kernel(p_ref, hp_ref, hn_ref, mu_ref, w0_ref, w2_ref, a0_ref, a2_ref, g2_ref, kk_ref, ka_ref,
                      rk_ref, r_o, v_o, kk_o, wf_o, wb_o, bf_o, bb_o, kf_o, kb_o, g_o, bonus_o, xs_ref):
    T = ROW_TILE
    i = pl.program_id(1)
    n_lat = pl.num_programs(1) - 1
    x = p_ref[...]
    W = x.shape[1]
    rowi = lax.broadcasted_iota(jnp.int32, (T, 1), 0)
    coli = lax.broadcasted_iota(jnp.int32, (1, W), 1)
    prev = pltpu.roll(x, 1, 0)
    nxt = pltpu.roll(x, T - 1, 0)

    @pl.when(i < n_lat)
    def _():
        gc = rowi & (GRID_W - 1)
        left = jnp.where(gc == 0, 0.0, prev)
        right = jnp.where(gc == GRID_W - 1, 0.0, nxt)
        hp = jnp.where(i == 0, 0.0, hp_ref[...])
        hn = jnp.where(i == n_lat - 1, 0.0, hn_ref[...])
        up = jnp.concatenate([hp, x[:T - GRID_W]], axis=0)
        down = jnp.concatenate([x[GRID_W:], hn], axis=0)
        q = W // 4
        xs_ref[...] = jnp.where(coli < q, left, jnp.where(coli < 2 * q, right, jnp.where(coli < 3 * q, up, down)))

    @pl.when(i == n_lat)
    def _():
        xs_ref[...] = jnp.where(coli < W // 2, jnp.where(rowi == 0, 0.0, prev), jnp.where(rowi == T - 1, 0.0, nxt))

    xm = x + (xs_ref[...] - x) * mu_ref[...]
    C = RWKV_HEADS * RWKV_N
    r = xm[:, 0:C]
    k = xm[:, C:2 * C]
    v = xm[:, 2 * C:3 * C]
    wd = jnp.tanh(xm[:, 3 * C:3 * C + 2 * DECAY_LORA]).astype(BF16)
    ad = xm[:, 3 * C + 2 * DECAY_LORA:3 * C + 2 * DECAY_LORA + 2 * AAA_LORA].astype(BF16)
    gd = _sigmoid(xm[:, 3 * C + 2 * DECAY_LORA + 2 * AAA_LORA:]).astype(BF16)
    bd = _block_diag_ones(RWKV_N)

    kk = k * kk_ref[...]
    nrm = jnp.sqrt(_group_sum(kk * kk, bd))
    kk = kk / jnp.maximum(nrm, 1e-12)
    u = w0_ref[...] + jnp.dot(wd, w2_ref[...], preferred_element_type=F32)
    decay = jnp.exp(-np.float32(np.exp(-0.5)) * _sigmoid(u))
    a = _sigmoid(a0_ref[...] + jnp.dot(ad, a2_ref[...], preferred_element_type=F32))
    ka = ka_ref[...]
    kd_f = k * (1.0 + (a[:, :C] - 1.0) * ka)
    kd_b = k * (1.0 + (a[:, C:] - 1.0) * ka)
    r_o[...] = r
    v_o[...] = v
    kk_o[...] = kk
    wf_o[...] = decay[:, :C]
    wb_o[...] = decay[:, C:]
    bf_o[...] = kk * a[:, :C]
    bb_o[...] = kk * a[:, C:]
    kf_o[...] = kd_f
    kb_o[...] = kd_b
    g_o[...] = jnp.dot(gd, g2_ref[...], preferred_element_type=F32)
    bonus_o[...] = _group_sum(r * (kd_f + kd_b) * rk_ref[...], bd) * v


def rwkv_prepare(p, mu, w0cat, w2cat, a0cat, a2cat, g2, k_k, k_a, r_k):
    b, tt, w = p.shape
    nt = tt // ROW_TILE
    n_lat = nt - 1
    hpt = ROW_TILE // GRID_W
    c = RWKV_HEADS * RWKV_N
    n_halo = tt // GRID_W
    const = lambda shape: pl.BlockSpec(shape, lambda bi, i: (0,) * len(shape))
    out_spec = pl.BlockSpec((None, ROW_TILE, c), lambda bi, i: (bi, i, 0))
    return pl.pallas_call(
        _rwkv_prep_kernel,
        grid=(b, nt),
        in_specs=[pl.BlockSpec((None, ROW_TILE, w), lambda bi, i: (bi, i, 0)),
                  pl.BlockSpec((None, GRID_W, w), lambda bi, i: (bi, jnp.maximum(i * hpt - 1, 0), 0)),
                  pl.BlockSpec((None, GRID_W, w), lambda bi, i: (bi, jnp.minimum(i * hpt + hpt, n_halo - 1), 0)),
                  const((1, w)), const((1, 2 * c)), const((2 * DECAY_LORA, 2 * c)), const((1, 2 * c)),
                  const((2 * AAA_LORA, 2 * c)), const((GATE_LORA, c)), const((1, c)), const((1, c)), const((1, c))],
        out_specs=[out_spec] * 11,
        out_shape=[jax.ShapeDtypeStruct((b, tt, c), F32)] * 11,
        scratch_shapes=[pltpu.VMEM((ROW_TILE, w), F32)],
        compiler_params=_cparams(("parallel", "parallel")),
        name="rwkv_prepare",
    )(p, p, p, mu, w0cat, w2cat, a0cat, a2cat, g2, k_k, k_a, r_k)


SCAN_STEPS = 16


def _rwkv_scan_kernel(w_ref, kn_ref, b_ref, k_ref, r_ref, v_ref, y_ref, s_ref, sa_ref):
    N = RWKV_N

    @pl.when(pl.program_id(0) == 0)
    def _():
        s_ref[...] = jnp.zeros_like(s_ref)
        sa_ref[...] = jnp.zeros_like(sa_ref)

    def step(t, carry):
        sa = sa_ref[...]
        vt = v_ref[t]
        acc_sa = [jnp.zeros((N, LANES), F32), jnp.zeros((N, LANES), F32)]
        acc_y = [jnp.zeros((N, LANES), F32), jnp.zeros((N, LANES), F32)]
        for kx in range(N):
            sk = s_ref[kx]
            sk = sk * w_ref[t, kx:kx + 1, :] - sa * b_ref[t, kx:kx + 1, :] + vt * k_ref[t, kx:kx + 1, :]
            s_ref[kx] = sk
            acc_sa[kx % 2] = acc_sa[kx % 2] + sk * kn_ref[t, kx:kx + 1, :]
            acc_y[kx % 2] = acc_y[kx % 2] + sk * r_ref[t, kx:kx + 1, :]
        sa_ref[...] = acc_sa[0] + acc_sa[1]
        y_ref[t] = acc_y[0] + acc_y[1]
        return carry

    lax.fori_loop(0, SCAN_STEPS, step, 0)


def rwkv_scan(w, kn, bb, kd, r, v):
    t, n, lanes = w.shape
    spec = pl.BlockSpec((SCAN_STEPS, n, lanes), lambda i: (i, 0, 0))
    return pl.pallas_call(
        _rwkv_scan_kernel,
        grid=(t // SCAN_STEPS,),
        in_specs=[spec] * 6,
        out_specs=spec,
        out_shape=jax.ShapeDtypeStruct((t, n, lanes), F32),
        scratch_shapes=[pltpu.VMEM((n, n, lanes), F32), pltpu.VMEM((n, lanes), F32)],
        compiler_params=_cparams(("arbitrary",)),
        name="rwkv_scan",
    )(w, kn, bb, kd, r, v)


def _merge_kernel(hf_ref, hb_ref, o_ref, y_ref, bonus_ref, g_ref, mg_ref, x_ref, mod_ref, gm_ref, lnw_ref,
                  lnb_ref, wpm_ref, wpr_ref, wout_ref, out_ref):
    c = RWKV_HEADS * RWKV_N
    hm = hf_ref[...] + hb_ref[...]
    ms = _group_sum(hm * hm, _block_diag_ones(MLSTM_DV)) * (1.0 / MLSTM_DV)
    hmn = hm * lax.rsqrt(ms + NORM_EPS) * gm_ref[...] * _sigmoid(o_ref[...])
    ym = jnp.dot(hmn.astype(BF16), wpm_ref[...], preferred_element_type=F32)
    bd = _block_diag_ones(RWKV_N)
    y = y_ref[...]
    yc = y - _group_sum(y, bd) * (1.0 / RWKV_N)
    var = _group_sum(yc * yc, bd) * (1.0 / RWKV_N)
    yn = yc * lax.rsqrt(var + GN_EPS) * lnw_ref[...] + lnb_ref[...]
    yr_in = ((yn + bonus_ref[...]) * g_ref[...]).astype(BF16)
    yr = jnp.dot(yr_in, wpr_ref[...], preferred_element_type=F32)
    mg = mg_ref[...]
    z = _sigmoid(mg[:, :c]) * ym + _sigmoid(mg[:, c:]) * yr
    yy = jnp.dot(z.astype(BF16), wout_ref[...], preferred_element_type=F32)
    out_ref[...] = x_ref[...] + mod_ref[GT1:GT1 + 1, :] * yy


def merge(hf, hb, qkvo, y, bonus, g, mg, x_all, modtab, gm, lnw, lnb, wpm, wpr, wout, nt_out):
    b, tt, d = x_all.shape
    ctx_tile = tt // ROW_TILE - 1
    row = lambda width, colblk=0: pl.BlockSpec((None, ROW_TILE, width), lambda bi, i: (bi, i, colblk))
    const = lambda shape: pl.BlockSpec(shape, lambda bi, i: (0,) * len(shape))
    return pl.pallas_call(
        _merge_kernel,
        grid=(b, nt_out),
        in_specs=[row(d), row(d), row(d, 2), row(d), row(d), row(d), row(2 * d), row(d),
                  pl.BlockSpec((None, None, MOD_ROWS, d), lambda bi, i: (bi, jnp.where(i == ctx_tile, 0, 1), 0, 0)),
                  const((1, d)), const((1, d)), const((1, d)), const((d, d)), const((d, d)), const((d, d))],
        out_specs=row(d),
        out_shape=jax.ShapeDtypeStruct((b, nt_out * ROW_TILE, d), F32),
        compiler_params=_cparams(("parallel", "parallel")),
        name="merge",
    )(hf, hb, qkvo, y, bonus, g, mg, x_all, modtab, gm, lnw, lnb, wpm, wpr, wout)


def _ffn_kernel(x_ref, g_ref, mod_ref, wg_ref, wu_ref, wd_ref, o_ref, h_scr, acc_scr):
    j = pl.program_id(2)

    @pl.when(j == 0)
    def _():
        h_scr[...] = _rms_mod(x_ref[...], g_ref[...], mod_ref[...], SH2, SC2).astype(BF16)
        acc_scr[...] = jnp.zeros_like(acc_scr)

    h = h_scr[...]
    a = jnp.dot(h, wg_ref[...], preferred_element_type=F32)
    u = jnp.dot(h, wu_ref[...], preferred_element_type=F32)
    t = (a * _sigmoid(a) * u).astype(BF16)
    acc_scr[...] += jnp.dot(t, wd_ref[...], preferred_element_type=F32)

    @pl.when(j == pl.num_programs(2) - 1)
    def _():
        o_ref[...] = x_ref[...] + mod_ref[GT2:GT2 + 1, :] * acc_scr[...]


def ffn_dense(x_all, g, modtab, wg, wu, wd, tf):
    b, tt, d = x_all.shape
    nt = tt // ROW_TILE
    ctx_tile = nt - 1
    f = wg.shape[1]
    row = pl.BlockSpec((None, ROW_TILE, d), lambda bi, i, j: (bi, i, 0))
    return pl.pallas_call(
        _ffn_kernel,
        grid=(b, nt, f // tf),
        in_specs=[row,
                  pl.BlockSpec((1, d), lambda bi, i, j: (0, 0)),
                  pl.BlockSpec((None, None, MOD_ROWS, d),
                               lambda bi, i, j: (bi, jnp.where(i == ctx_tile, 0, 1), 0, 0)),
                  pl.BlockSpec((d, tf), lambda bi, i, j: (0, j)),
                  pl.BlockSpec((d, tf), lambda bi, i, j: (0, j)),
                  pl.BlockSpec((tf, d), lambda bi, i, j: (j, 0))],
        out_specs=row,
        out_shape=jax.ShapeDtypeStruct((b, tt, d), F32),
        scratch_shapes=[pltpu.VMEM((ROW_TILE, d), BF16), pltpu.VMEM((ROW_TILE, d), F32)],
        compiler_params=_cparams(("parallel", "parallel", "arbitrary")),
        name="ffn_dense",
    )(x_all, g.reshape(1, d), modtab, wg, wu, wd)


MOE_ROWS = 512


def _moe_kernel(x_ref, g_ref, mod_ref, wr_ref, wg_ref, wu_ref, wd_ref, gfin_ref, o_ref, h_scr, acc_scr, gate_scr):
    e = pl.program_id(2)
    j = pl.program_id(3)
    first = jnp.logical_and(e == 0, j == 0)
    last = jnp.logical_and(e == pl.num_programs(2) - 1, j == pl.num_programs(3) - 1)

    @pl.when(first)
    def _():
        h = _rms_mod(x_ref[...], g_ref[...], mod_ref[...], SH2, SC2)
        h_scr[...] = h.astype(BF16)
        acc_scr[...] = jnp.zeros_like(acc_scr)
        logits = jnp.dot(h, wr_ref[...], precision=HIGHEST, preferred_element_type=F32)
        lane = lax.broadcasted_iota(jnp.int32, logits.shape, 1)
        logits = jnp.where(lane < N_EXPERTS, logits, -jnp.inf)
        m1 = jnp.max(logits, axis=-1, keepdims=True)
        i1 = jnp.min(jnp.where(logits == m1, lane, LANES), axis=-1, keepdims=True)
        rest = jnp.where(lane == i1, -jnp.inf, logits)
        m2 = jnp.max(rest, axis=-1, keepdims=True)
        i2 = jnp.min(jnp.where(rest == m2, lane, LANES), axis=-1, keepdims=True)
        e21 = jnp.exp(m2 - m1)
        p1 = 1.0 / (1.0 + e21)
        gate_scr[...] = jnp.where(lane == i1, p1, jnp.where(lane == i2, e21 * p1, 0.0))

    h = h_scr[...]
    a = jnp.dot(h, wg_ref[...], preferred_element_type=F32)
    u = jnp.dot(h, wu_ref[...], preferred_element_type=F32)
    lane = lax.broadcasted_iota(jnp.int32, gate_scr.shape, 1)
    gate = jnp.sum(jnp.where(lane == e, gate_scr[...], 0.0), axis=-1, keepdims=True)
    t = (a * _sigmoid(a) * u).astype(BF16)
    acc_scr[...] += gate * jnp.dot(t, wd_ref[...], preferred_element_type=F32)

    @pl.when(last)
    def _():
        xo = x_ref[...] + mod_ref[GT2:GT2 + 1, :] * acc_scr[...]
        ms = jnp.mean(xo * xo, axis=-1, keepdims=True)
        o_ref[...] = xo * lax.rsqrt(ms + NORM_EPS) * gfin_ref[...]


def moe_final(x_lat, g, modtab, w_router, wg, wu, wd, g_final, tf):
    b, s, d = x_lat.shape
    ne, _, f = wg.shape
    row = pl.BlockSpec((None, MOE_ROWS, d), lambda bi, i, e, j: (bi, i, 0))
    return pl.pallas_call(
        _moe_kernel,
        grid=(b, s // MOE_ROWS, ne, f // tf),
        in_specs=[row,
                  pl.BlockSpec((1, d), lambda bi, i, e, j: (0, 0)),
                  pl.BlockSpec((None, None, MOD_ROWS, d), lambda bi, i, e, j: (bi, 1, 0, 0)),
                  pl.BlockSpec((d, LANES), lambda bi, i, e, j: (0, 0)),
                  pl.BlockSpec((None, d, tf), lambda bi, i, e, j: (e, 0, j)),
                  pl.BlockSpec((None, d, tf), lambda bi, i, e, j: (e, 0, j)),
                  pl.BlockSpec((None, tf, d), lambda bi, i, e, j: (e, j, 0)),
                  pl.BlockSpec((1, d), lambda bi, i, e, j: (0, 0))],
        out_specs=row,
        out_shape=jax.ShapeDtypeStruct((b, s, d), F32),
        scratch_shapes=[pltpu.VMEM((MOE_ROWS, d), BF16), pltpu.VMEM((MOE_ROWS, d), F32),
                        pltpu.VMEM((MOE_ROWS, LANES), F32)],
        compiler_params=_cparams(("parallel", "parallel", "arbitrary", "arbitrary")),
        name="moe_final",
    )(x_lat, g.reshape(1, d), modtab, w_router, wg, wu, wd, g_final.reshape(1, d))


def _scan_order(a, seq, rev):
    lat, ctx = a[:, :seq], a[:, seq:]
    if rev:
        lat, ctx = lat[:, ::-1], ctx[:, ::-1]
    return jnp.concatenate([ctx, lat], axis=1)


def _to_scan_layout(a_f, a_b, seq):
    b, tt, _ = a_f.shape
    st = jnp.stack([_scan_order(a_f, seq, False), _scan_order(a_b, seq, True)], axis=0)
    st = st.reshape(2, b, tt, RWKV_HEADS, RWKV_N)
    return st.transpose(2, 4, 0, 1, 3).reshape(tt, RWKV_N, 2 * b * RWKV_HEADS)


def _from_scan_layout(y, b, seq):
    tt = y.shape[0]
    st = y.reshape(tt, RWKV_N, 2, b, RWKV_HEADS).transpose(2, 3, 0, 4, 1).reshape(2, b, tt, RWKV_HEADS * RWKV_N)
    ctx_len = tt - seq
    yf = jnp.concatenate([st[0][:, ctx_len:], st[0][:, :ctx_len]], axis=1)
    yb = jnp.concatenate([st[1][:, ctx_len:][:, ::-1], st[1][:, :ctx_len][:, ::-1]], axis=1)
    return yf + yb


def kernel(x, c, ctx, c_ctx, w_ada, b_ada, g_norm_mix, g_norm_ffn, w_in, b_mlstm_gate, g_mlstm_norm, mu_shift, w0, w2, a0, a2, g2, k_k, k_a, r_k, ln_w, ln_b, w_proj_mlstm, w_proj_rwkv, w_out, w_ff_gate, w_ff_up, w_ff_down, w_router, w_exp_gate, w_exp_up, w_exp_down, g_final):
    b, seq, d = x.shape
    ctx_len = ctx.shape[1]
    depth = w_ada.shape[0]
    assert ctx_len == ROW_TILE and seq % ROW_TILE == 0 and seq % MOE_ROWS == 0 and 2 * b * RWKV_HEADS == LANES
    tt = seq + ctx_len
    m_all = b * tt
    H = MLSTM_HEADS
    cr = RWKV_HEADS * RWKV_N
    hq, hv = H * MLSTM_DQK, H * MLSTM_DV

    x_all = jnp.concatenate([x, ctx], axis=1)
    cond = jnp.concatenate([c, c_ctx[None, :], jnp.zeros((MOD_ROWS - b - 1, d), F32)], axis=0)
    mod_all = ada_mod(cond, w_ada, b_ada).reshape(depth, MOD_ROWS, 6, d)

    o_gate = 2 * hq + 2 * hv
    o_rwkv = o_gate + 4 * H
    rwkv_w = 3 * cr + 2 * DECAY_LORA + 2 * AAA_LORA + GATE_LORA
    o_mg = o_rwkv + rwkv_w

    out = None
    for l in range(depth):
        need_ctx = l < depth - 1
        pad = jnp.zeros((b, MOD_ROWS - 6, d), F32)
        mod_lat = jnp.concatenate([mod_all[l, :b], pad], axis=1)
        mod_ctx = jnp.broadcast_to(jnp.concatenate([mod_all[l, b], pad[0]], axis=0), (b, MOD_ROWS, d))
        modtab = jnp.stack([mod_ctx, mod_lat], axis=1)

        wl = w_in[l]
        w_qkvo = wl[:, :o_gate].astype(BF16)
        wgt = wl[:, o_gate:o_rwkv].reshape(d, 4, H)
        zpad = jnp.zeros((d, LANES - 2 * H), F32)
        w_gates = jnp.concatenate([wgt[:, 0], wgt[:, 2], zpad, wgt[:, 1], wgt[:, 3], zpad], axis=1)
        bg = b_mlstm_gate[l]
        lpad = jnp.zeros((LANES - 2 * H,), F32)
        bias_i = jnp.concatenate([bg[0], bg[2], lpad]).reshape(1, LANES)
        bias_f = jnp.concatenate([bg[1], bg[3], lpad]).reshape(1, LANES)
        w_rwkv = wl[:, o_rwkv:o_mg].astype(BF16)
        w_mg = wl[:, o_mg:].astype(BF16)

        hb16, hf32 = norm_mod(x_all, g_norm_mix[l], modtab)
        hb16 = hb16.reshape(m_all, d)
        qkvo = matmul(hb16, w_qkvo, 512, 1024).reshape(b, tt, o_gate)
        gates = matmul(hf32.reshape(m_all, d), w_gates, 512, 2 * LANES, precision=HIGHEST).reshape(b, tt, 2 * LANES)
        prw = matmul(hb16, w_rwkv, 512, rwkv_w // 3).reshape(b, tt, rwkv_w)
        mg = matmul(hb16, w_mg, 512, 1024).reshape(b, tt, 2 * d)

        h_f = mlstm_scan(qkvo, gates, bias_i, bias_f, rev=False)
        h_b = mlstm_scan(qkvo, gates, bias_i, bias_f, rev=True)

        zl = jnp.zeros((DECAY_LORA, cr), F32)
        w2cat = jnp.concatenate([jnp.concatenate([w2[l, 0], zl], axis=1),
                                 jnp.concatenate([zl, w2[l, 1]], axis=1)], axis=0).astype(BF16)
        a2cat = jnp.concatenate([jnp.concatenate([a2[l, 0], zl], axis=1),
                                 jnp.concatenate([zl, a2[l, 1]], axis=1)], axis=0).astype(BF16)
        r_, v_, kk_, wf_, wb_, bf_, bb_, kf_, kb_, g_, bonus_ = rwkv_prepare(
            prw, mu_shift[l].reshape(1, rwkv_w), w0[l].reshape(1, 2 * cr), w2cat, a0[l].reshape(1, 2 * cr), a2cat,
            g2[l].astype(BF16), k_k[l].reshape(1, cr), k_a[l].reshape(1, cr), r_k[l].reshape(1, cr))
        kk_s = _to_scan_layout(kk_, kk_, seq)
        kn_s = jnp.concatenate([kk_s[1:], jnp.zeros_like(kk_s[:1])], axis=0)
        y_s = rwkv_scan(_to_scan_layout(wf_, wb_, seq), kn_s, _to_scan_layout(bf_, bb_, seq),
                        _to_scan_layout(kf_, kb_, seq), _to_scan_layout(r_, r_, seq), _to_scan_layout(v_, v_, seq))
        y_ = _from_scan_layout(y_s, b, seq)

        nt_out = tt // ROW_TILE if need_ctx else seq // ROW_TILE
        x_mid = merge(h_f, h_b, qkvo, y_, bonus_, g_, mg, x_all, modtab, g_mlstm_norm[l].reshape(1, hv),
                      ln_w[l].reshape(1, cr), ln_b[l].reshape(1, cr), w_proj_mlstm[l].astype(BF16),
                      w_proj_rwkv[l].astype(BF16), w_out[l].astype(BF16), nt_out)

        i = l // 2
        if l % 2 == 0:
            assert need_ctx
            x_all = ffn_dense(x_mid, g_norm_ffn[l], modtab, w_ff_gate[i].astype(BF16), w_ff_up[i].astype(BF16),
                              w_ff_down[i].astype(BF16), w_ff_gate.shape[2] // 2)
        else:
            assert not need_ctx
            wr = jnp.concatenate([w_router[i], jnp.zeros((d, LANES - N_EXPERTS), F32)], axis=1)
            out = moe_final(x_mid, g_norm_ffn[l], modtab, wr, w_exp_gate[i].astype(BF16), w_exp_up[i].astype(BF16),
                            w_exp_down[i].astype(BF16), g_final, w_exp_gate.shape[3] // 2)
    return out
```

```python
import functools

import jax
import jax.numpy as jnp
import numpy as np
from jax import lax
from jax.experimental import pallas as pl
from jax.experimental.pallas import tpu as pltpu

F32 = jnp.float32
BF16 = jnp.bfloat16
HIGHEST = lax.Precision.HIGHEST

GRID_W = 64
MLSTM_HEADS = 8
MLSTM_DQK = 64
MLSTM_DV = 128
GATE_SOFT_CAP = 15.0
RWKV_HEADS = 16
RWKV_N = 64
DECAY_LORA = 64
AAA_LORA = 64
GATE_LORA = 128
GN_EPS = 64e-5
N_EXPERTS = 8
NORM_EPS = 1e-6

LANES = 128
MXU_DIM = 256
ROW_TILE = 256
VMEM_LIMIT = 56 * 1024 * 1024

SH1, SC1, GT1, SH2, SC2, GT2 = range(6)
MOD_ROWS = 8


def _cparams(sem):
    return pltpu.CompilerParams(dimension_semantics=sem, vmem_limit_bytes=VMEM_LIMIT)


def _sigmoid(x):
    return 1.0 / (1.0 + jnp.exp(-x))


def _block_diag_ones(block):
    r = lax.broadcasted_iota(jnp.int32, (MXU_DIM, MXU_DIM), 0)
    c = lax.broadcasted_iota(jnp.int32, (MXU_DIM, MXU_DIM), 1)
    sh = int(np.log2(block))
    return jnp.where((r >> sh) == (c >> sh), 1.0, 0.0).astype(BF16)


def _group_sum(x, bd):
    outs = []
    for c in range(x.shape[1] // MXU_DIM):
        xs = x[:, c * MXU_DIM:(c + 1) * MXU_DIM]
        hi = xs.astype(BF16)
        r1 = xs - hi.astype(F32)
        mid = r1.astype(BF16)
        lo = (r1 - mid.astype(F32)).astype(BF16)
        acc = jnp.dot(hi, bd, preferred_element_type=F32)
        acc += jnp.dot(mid, bd, preferred_element_type=F32)
        acc += jnp.dot(lo, bd, preferred_element_type=F32)
        outs.append(acc)
    return jnp.concatenate(outs, axis=1)


def _rms_mod(x, g, mod, shift_row, scale_row):
    ms = jnp.mean(x * x, axis=-1, keepdims=True)
    y = x * lax.rsqrt(ms + NORM_EPS) * g
    return y * (1.0 + mod[scale_row:scale_row + 1, :]) + mod[shift_row:shift_row + 1, :]


def _ada_kernel(s_ref, w_ref, b_ref, o_ref):
    s = s_ref[...]
    s = s * _sigmoid(s)
    o_ref[...] = jnp.dot(s, w_ref[...], precision=HIGHEST, preferred_element_type=F32) + b_ref[...]


def ada_mod(cond, w_ada, b_ada):
    depth, d, n = w_ada.shape
    tn = n // 4
    return pl.pallas_call(
        _ada_kernel,
        grid=(depth, n // tn),
        in_specs=[pl.BlockSpec((MOD_ROWS, d), lambda l, j: (0, 0)),
                  pl.BlockSpec((None, d, tn), lambda l, j: (l, 0, j)),
                  pl.BlockSpec((None, 1, tn), lambda l, j: (l, 0, j))],
        out_specs=pl.BlockSpec((None, MOD_ROWS, tn), lambda l, j: (l, 0, j)),
        out_shape=jax.ShapeDtypeStruct((depth, MOD_ROWS, n), F32),
        compiler_params=_cparams(("arbitrary", "arbitrary")),
        name="ada_mod",
    )(cond, w_ada, b_ada.reshape(depth, 1, n))


def _norm_mod_kernel(x_ref, g_ref, mod_ref, hb_ref, hf_ref):
    h = _rms_mod(x_ref[...], g_ref[...], mod_ref[...], SH1, SC1)
    hb_ref[...] = h.astype(BF16)
    hf_ref[...] = h


def norm_mod(x_all, g, modtab):
    b, tt, d = x_all.shape
    nt = tt // ROW_TILE
    ctx_tile = nt - 1
    spec = pl.BlockSpec((None, ROW_TILE, d), lambda bi, i: (bi, i, 0))
    return pl.pallas_call(
        _norm_mod_kernel,
        grid=(b, nt),
        in_specs=[spec,
                  pl.BlockSpec((1, d), lambda bi, i: (0, 0)),
                  pl.BlockSpec((None, None, MOD_ROWS, d), lambda bi, i: (bi, jnp.where(i == ctx_tile, 0, 1), 0, 0))],
        out_specs=[spec, spec],
        out_shape=[jax.ShapeDtypeStruct((b, tt, d), BF16), jax.ShapeDtypeStruct((b, tt, d), F32)],
        compiler_params=_cparams(("parallel", "parallel")),
        name="norm_mod",
    )(x_all, g.reshape(1, d), modtab)


def _mm_kernel(a_ref, w_ref, o_ref, *, precision):
    o_ref[...] = jnp.dot(a_ref[...], w_ref[...], precision=precision,
                         preferred_element_type=F32).astype(o_ref.dtype)


def matmul(a, w, tm, tn, out_dtype=F32, precision=None):
    m, k = a.shape
    n = w.shape[1]
    return pl.pallas_call(
        functools.partial(_mm_kernel, precision=precision),
        grid=(n // tn, m // tm),
        in_specs=[pl.BlockSpec((tm, k), lambda j, i: (i, 0)),
                  pl.BlockSpec((k, tn), lambda j, i: (0, j))],
        out_specs=pl.BlockSpec((tm, tn), lambda j, i: (i, j)),
        out_shape=jax.ShapeDtypeStruct((m, n), out_dtype),
        compiler_params=_cparams(("parallel", "parallel")),
        name="matmul",
    )(a, w)


def _mlstm_kernel(q_ref, k_ref, v_ref, gi_ref, gf_ref, bi_ref, bf_ref, o_ref, ct_ref, m_ref, *, rev):
    L = ROW_TILE
    H, DQK, DV = MLSTM_HEADS, MLSTM_DQK, MLSTM_DV

    @pl.when(pl.program_id(1) == 0)
    def _():
        ct_ref[...] = jnp.zeros_like(ct_ref)
        m_ref[...] = jnp.zeros_like(m_ref)

    def cap(a):
        return GATE_SOFT_CAP * jnp.tanh(a / GATE_SOFT_CAP)

    ic_all = cap(gi_ref[...] + bi_ref[...])
    fp = cap(gf_ref[...] + bf_ref[...])
    lf_all = jnp.minimum(fp, 0.0) - jnp.log(1.0 + jnp.exp(-jnp.abs(fp)))
    row = lax.broadcasted_iota(jnp.int32, (L, L), 0)
    col = lax.broadcasted_iota(jnp.int32, (L, L), 1)
    tri = (col >= row) if rev else (col <= row)
    b_all = jnp.dot(jnp.where(tri, 1.0, 0.0), lf_all, precision=HIGHEST, preferred_element_type=F32)
    src_all = ic_all - b_all
    src_t = src_all.T
    last = 0 if rev else L - 1
    lane0 = lax.broadcasted_iota(jnp.int32, (L, LANES), 1) == 0
    ones_blk = jnp.where(lane0, 1.0, 0.0).astype(BF16)
    scale = DQK ** -0.5

    for h in range(H):
        ln = (H if rev else 0) + h
        bcol = b_all[:, ln:ln + 1]
        iccol = ic_all[:, ln:ln + 1]
        srow = src_t[ln:ln + 1, :]
        m_prev = m_ref[h][0:1, 0:1]
        dmat = jnp.where(tri, bcol + srow, -jnp.inf)
        inter = bcol + m_prev
        m_t = jnp.maximum(inter, jnp.max(dmat, axis=-1, keepdims=True))
        w_intra = jnp.exp(dmat - m_t)
        w_inter = jnp.exp(inter - m_t)
        qb = (q_ref[:, h * DQK:(h + 1) * DQK] * scale).astype(BF16)
        kf = k_ref[:, h * DQK:(h + 1) * DQK]
        kb = kf.astype(BF16)
        vaug = jnp.concatenate([v_ref[:, h * DV:(h + 1) * DV].astype(BF16), ones_blk], axis=1)
        sqk = lax.dot_general(qb, kb, (((1,), (1,)), ((), ())), preferred_element_type=F32)
        sw = (sqk * w_intra).astype(BF16)
        ct = ct_ref[h]
        num_aug = (jnp.dot(sw, vaug, preferred_element_type=F32)
                   + w_inter * jnp.dot(qb, ct.astype(BF16), preferred_element_type=F32))
        num = num_aug[:, :DV]
        den = num_aug[:, DV:DV + 1]
        o_ref[:, h * DV:(h + 1) * DV] = num / jnp.maximum(jnp.abs(den), jnp.exp(-m_t))
        b_last = bcol[last:last + 1, :]
        w_src = b_last - bcol + iccol
        m_new = jnp.maximum(b_last + m_prev, jnp.max(w_src, axis=0, keepdims=True))
        a_src = jnp.exp(w_src - m_new)
        a_old = jnp.exp(b_last + m_prev - m_new)
        ks = (kf * a_src).astype(BF16)
        upd = lax.dot_general(ks, vaug, (((0,), (0,)), ((), ())), preferred_element_type=F32)
        ct_ref[h] = a_old * ct + upd
        m_ref[h] = jnp.broadcast_to(m_new, m_ref.shape[1:])


def mlstm_scan(qkvo, gates, bias_i, bias_f, rev):
    b, tt, _ = qkvo.shape
    nt = tt // ROW_TILE
    ctx_tile = nt - 1
    hq = MLSTM_HEADS * MLSTM_DQK
    hv = MLSTM_HEADS * MLSTM_DV
    if rev:
        order = lambda s: jnp.where(s == 0, ctx_tile, ctx_tile - s)
    else:
        order = lambda s: jnp.where(s == 0, ctx_tile, s - 1)
    return pl.pallas_call(
        functools.partial(_mlstm_kernel, rev=rev),
        grid=(b, nt),
        in_specs=[pl.BlockSpec((None, ROW_TILE, hq), lambda bi, s: (bi, order(s), 0)),
                  pl.BlockSpec((None, ROW_TILE, hq), lambda bi, s: (bi, order(s), 1)),
                  pl.BlockSpec((None, ROW_TILE, hv), lambda bi, s: (bi, order(s), 1)),
                  pl.BlockSpec((None, ROW_TILE, LANES), lambda bi, s: (bi, order(s), 0)),
                  pl.BlockSpec((None, ROW_TILE, LANES), lambda bi, s: (bi, order(s), 1)),
                  pl.BlockSpec((1, LANES), lambda bi, s: (0, 0)),
                  pl.BlockSpec((1, LANES), lambda bi, s: (0, 0))],
        out_specs=pl.BlockSpec((None, ROW_TILE, hv), lambda bi, s: (bi, order(s), 0)),
        out_shape=jax.ShapeDtypeStruct((b, tt, hv), F32),
        scratch_shapes=[pltpu.VMEM((MLSTM_HEADS, MLSTM_DQK, 2 * MLSTM_DV), F32),
                        pltpu.VMEM((MLSTM_HEADS, 8, LANES), F32)],
        compiler_params=_cparams(("parallel", "arbitrary")),
        name="mlstm_rev" if rev else "mlstm_fwd",
    )(qkvo, qkvo, qkvo, gates, gates, bias_i, bias_f)


OP_KK, OP_R, OP_WF, OP_BF, OP_KF, OP_WB, OP_BB, OP_KB = range(8)
N_OPS = 8


def _rwkv_prep_kernel(p_ref, hp_ref, hn_ref, mu_ref, w0_ref, w2_ref, a0_ref, a2_ref, g2_ref, kk_ref, ka_ref,
                      rk_ref, ops_o, v_o, g_o, bonus_o, xs_ref):
    T = ROW_TILE
    i = pl.program_id(1)
    n_lat = pl.num_programs(1) - 1
    x = p_ref[...]
    W = x.shape[1]
    rowi = lax.broadcasted_iota(jnp.int32, (T, 1), 0)
    coli = lax.broadcasted_iota(jnp.int32, (1, W), 1)
    prev = pltpu.roll(x, 1, 0)
    nxt = pltpu.roll(x, T - 1, 0)

    @pl.when(i < n_lat)
    def _():
        gc = rowi & (GRID_W - 1)
        left = jnp.where(gc == 0, 0.0, prev)
        right = jnp.where(gc == GRID_W - 1, 0.0, nxt)
        hp = jnp.where(i == 0, 0.0, hp_ref[...])
        hn = jnp.where(i == n_lat - 1, 0.0, hn_ref[...])
        up = jnp.concatenate([hp, x[:T - GRID_W]], axis=0)
        down = jnp.concatenate([x[GRID_W:], hn], axis=0)
        q = W // 4
        xs_ref[...] = jnp.where(coli < q, left, jnp.where(coli < 2 * q, right, jnp.where(coli < 3 * q, up, down)))

    @pl.when(i == n_lat)
    def _():
        xs_ref[...] = jnp.where(coli < W // 2, jnp.where(rowi == 0, 0.0, prev), jnp.where(rowi == T - 1, 0.0, nxt))

    xm = x + (xs_ref[...] - x) * mu_ref[...]
    C = RWKV_HEADS * RWKV_N
    r = xm[:, 0:C]
    k = xm[:, C:2 * C]
    v = xm[:, 2 * C:3 * C]
    wd = jnp.tanh(xm[:, 3 * C:3 * C + 2 * DECAY_LORA]).astype(BF16)
    ad = xm[:, 3 * C + 2 * DECAY_LORA:3 * C + 2 * DECAY_LORA + 2 * AAA_LORA].astype(BF16)
    gd = _sigmoid(xm[:, 3 * C + 2 * DECAY_LORA + 2 * AAA_LORA:]).astype(BF16)
    bd = _block_diag_ones(RWKV_N)

    kk = k * kk_ref[...]
    nrm = jnp.sqrt(_group_sum(kk * kk, bd))
    kk = kk / jnp.maximum(nrm, 1e-12)
    u = w0_ref[...] + jnp.dot(wd, w2_ref[...], preferred_element_type=F32)
    decay = jnp.exp(-np.float32(np.exp(-0.5)) * _sigmoid(u))
    a = _sigmoid(a0_ref[...] + jnp.dot(ad, a2_ref[...], preferred_element_type=F32))
    ka = ka_ref[...]
    kd_f = k * (1.0 + (a[:, :C] - 1.0) * ka)
    kd_b = k * (1.0 + (a[:, C:] - 1.0) * ka)
    ops_o[OP_R] = r
    v_o[...] = v
    ops_o[OP_KK] = kk
    ops_o[OP_WF] = decay[:, :C]
    ops_o[OP_WB] = decay[:, C:]
    ops_o[OP_BF] = kk * a[:, :C]
    ops_o[OP_BB] = kk * a[:, C:]
    ops_o[OP_KF] = kd_f
    ops_o[OP_KB] = kd_b
    g_o[...] = jnp.dot(gd, g2_ref[...], preferred_element_type=F32)
    bonus_o[...] = _group_sum(r * (kd_f + kd_b) * rk_ref[...], bd) * v


def rwkv_prepare(p, mu, w0cat, w2cat, a0cat, a2cat, g2, k_k, k_a, r_k):
    b, tt, w = p.shape
    nt = tt // ROW_TILE
    n_lat = nt - 1
    hpt = ROW_TILE // GRID_W
    c = RWKV_HEADS * RWKV_N
    n_halo = tt // GRID_W
    const = lambda shape: pl.BlockSpec(shape, lambda bi, i: (0,) * len(shape))
    out_spec = pl.BlockSpec((None, ROW_TILE, c), lambda bi, i: (bi, i, 0))
    return pl.pallas_call(
        _rwkv_prep_kernel,
        grid=(b, nt),
        in_specs=[pl.BlockSpec((None, ROW_TILE, w), lambda bi, i: (bi, i, 0)),
                  pl.BlockSpec((None, GRID_W, w), lambda bi, i: (bi, jnp.maximum(i * hpt - 1, 0), 0)),
                  pl.BlockSpec((None, GRID_W, w), lambda bi, i: (bi, jnp.minimum(i * hpt + hpt, n_halo - 1), 0)),
                  const((1, w)), const((1, 2 * c)), const((2 * DECAY_LORA, 2 * c)), const((1, 2 * c)),
                  const((2 * AAA_LORA, 2 * c)), const((GATE_LORA, c)), const((1, c)), const((1, c)), const((1, c))],
        out_specs=[pl.BlockSpec((N_OPS, None, ROW_TILE, c), lambda bi, i: (0, bi, i, 0))] + [out_spec] * 3,
        out_shape=[jax.ShapeDtypeStruct((N_OPS, b, tt, c), F32)] + [jax.ShapeDtypeStruct((b, tt, c), F32)] * 3,
        scratch_shapes=[pltpu.VMEM((ROW_TILE, w), F32)],
        compiler_params=_cparams(("parallel", "parallel")),
        name="rwkv_prepare",
    )(p, p, p, mu, w0cat, w2cat, a0cat, a2cat, g2, k_k, k_a, r_k)


RELAYOUT_ROWS = 128
V_HALF = RWKV_N // 2


def _rows_to_heads(x_ref, zs_ref):
    for b in range(x_ref.shape[0]):
        for cb in range(x_ref.shape[2] // LANES):
            r0 = (b * (x_ref.shape[2] // LANES) + cb) * LANES
            zs_ref[r0:r0 + LANES, :] = x_ref[b, :, cb * LANES:(cb + 1) * LANES].T


def _to_scan_k_kernel(x_ref, o_ref, zs_ref):
    _rows_to_heads(x_ref, zs_ref)
    nbh = zs_ref.shape[0] // RWKV_N
    for k in range(RWKV_N):
        g = zs_ref[pl.ds(k, nbh, stride=RWKV_N), :]
        o_ref[k] = jnp.concatenate([g, g], axis=0).T


def to_scan_k(ops):
    n_ops, b, tt, c = ops.shape
    return pl.pallas_call(
        _to_scan_k_kernel,
        grid=(n_ops, tt // RELAYOUT_ROWS),
        in_specs=[pl.BlockSpec((None, b, RELAYOUT_ROWS, c), lambda o, i: (o, 0, i, 0))],
        out_specs=pl.BlockSpec((None, RWKV_N, RELAYOUT_ROWS, LANES), lambda o, i: (o, 0, i, 0)),
        out_shape=jax.ShapeDtypeStruct((n_ops, RWKV_N, tt, LANES), F32),
        scratch_shapes=[pltpu.VMEM((b * c, RELAYOUT_ROWS), F32)],
        compiler_params=_cparams(("parallel", "parallel")),
        name="to_scan_k",
    )(ops)


def _to_scan_v_kernel(x_ref, o_ref, zs_ref):
    _rows_to_heads(x_ref, zs_ref)
    nbh = zs_ref.shape[0] // RWKV_N
    for v in range(V_HALF):
        m = jnp.concatenate([zs_ref[pl.ds(v, nbh, stride=RWKV_N), :],
                             zs_ref[pl.ds(V_HALF + v, nbh, stride=RWKV_N), :]], axis=0)
        o_ref[:, v, :] = m.T


def to_scan_v(x):
    b, tt, c = x.shape
    return pl.pallas_call(
        _to_scan_v_kernel,
        grid=(tt // RELAYOUT_ROWS,),
        in_specs=[pl.BlockSpec((b, RELAYOUT_ROWS, c), lambda i: (0, i, 0))],
        out_specs=pl.BlockSpec((RELAYOUT_ROWS, V_HALF, LANES), lambda i: (i, 0, 0)),
        out_shape=jax.ShapeDtypeStruct((tt, V_HALF, LANES), F32),
        scratch_shapes=[pltpu.VMEM((b * c, RELAYOUT_ROWS), F32)],
        compiler_params=_cparams(("parallel",)),
        name="to_scan_v",
    )(x)


def _from_scan_kernel(yf_ref, yb_ref, o_ref, zs_ref):
    nbh = zs_ref.shape[0] // RWKV_N
    for v in range(V_HALF):
        t = (yf_ref[:, v, :] + yb_ref[:, v, :]).T
        zs_ref[pl.ds(v, nbh, stride=RWKV_N), :] = t[:nbh]
        zs_ref[pl.ds(V_HALF + v, nbh, stride=RWKV_N), :] = t[nbh:]
    for b in range(o_ref.shape[0]):
        for cb in range(o_ref.shape[2] // LANES):
            r0 = (b * (o_ref.shape[2] // LANES) + cb) * LANES
            o_ref[b, :, cb * LANES:(cb + 1) * LANES] = zs_ref[r0:r0 + LANES, :].T


def from_scan(yf, yb, b):
    tt = yf.shape[0]
    c = RWKV_HEADS * RWKV_N
    spec = pl.BlockSpec((RELAYOUT_ROWS, V_HALF, LANES), lambda i: (i, 0, 0))
    return pl.pallas_call(
        _from_scan_kernel,
        grid=(tt // RELAYOUT_ROWS,),
        in_specs=[spec, spec],
        out_specs=pl.BlockSpec((b, RELAYOUT_ROWS, c), lambda i: (0, i, 0)),
        out_shape=jax.ShapeDtypeStruct((b, tt, c), F32),
        scratch_shapes=[pltpu.VMEM((b * c, RELAYOUT_ROWS), F32)],
        compiler_params=_cparams(("parallel",)),
        name="from_scan",
    )(yf, yb)


SCAN_ROWS = 32


def _rwkv_scan_kernel(kk_f, kkn_f, r_f, w_f, b_f, k_f, kk_b, kkn_b, r_b, w_b, b_b, k_b, v_f, v_b, y_f, y_b,
                      s_ref, sa_ref, kkx_f, kkx_b):
    N, TB, PAD = RWKV_N, SCAN_ROWS, 8

    @pl.when(pl.program_id(0) == 0)
    def _():
        s_ref[...] = jnp.zeros_like(s_ref)
        sa_ref[...] = jnp.zeros_like(sa_ref)

    kkx_f[:, 0:TB, :] = kk_f[...]
    kkx_f[:, TB:TB + PAD, :] = kkn_f[:, 0:PAD, :]
    kkx_b[:, PAD:PAD + TB, :] = kk_b[...]
    kkx_b[:, 0:PAD, :] = kkn_b[:, TB - PAD:TB, :]

    dirs = ((kkx_f, r_f, w_f, b_f, k_f, v_f, y_f), (kkx_b, r_b, w_b, b_b, k_b, v_b, y_b))

    def step(j, carry):
        for d, (kkx_r, r_r, w_r, b_r, k_r, v_r, y_r) in enumerate(dirs):
            row, nxt = (j, j + 1) if d == 0 else (TB - 1 - j, PAD + TB - 2 - j)
            op = lambda ref, kx, rw: ref[kx, pl.ds(rw, 1), :]
            sa = sa_ref[d]
            vt = v_r[row]
            acc_sa = [jnp.zeros((V_HALF, LANES), F32), jnp.zeros((V_HALF, LANES), F32)]
            acc_y = [jnp.zeros((V_HALF, LANES), F32), jnp.zeros((V_HALF, LANES), F32)]
            for kx in range(N):
                sk = s_ref[d, kx] * op(w_r, kx, row) - sa * op(b_r, kx, row) + vt * op(k_r, kx, row)
                s_ref[d, kx] = sk
                acc_sa[kx % 2] = acc_sa[kx % 2] + sk * op(kkx_r, kx, nxt)
                acc_y[kx % 2] = acc_y[kx % 2] + sk * op(r_r, kx, row)
            sa_ref[d] = acc_sa[0] + acc_sa[1]
            y_r[row] = acc_y[0] + acc_y[1]
        return carry

    lax.fori_loop(0, SCAN_ROWS, step, 0)


def rwkv_scan(ops_s, v_s, seq):
    _, n, tt, lanes = ops_s.shape
    n_lat = seq // SCAN_ROWS
    n_blk = tt // SCAN_ROWS
    n_ctx = n_blk - n_lat
    blk_f = lambda s: jnp.where(s < n_ctx, n_lat + s, s - n_ctx)
    blk_b = lambda s: n_blk - 1 - s
    kspec = lambda o, blk: pl.BlockSpec((None, n, SCAN_ROWS, lanes), lambda s: (o, 0, blk(s), 0))
    vspec = lambda blk: pl.BlockSpec((SCAN_ROWS, V_HALF, lanes), lambda s: (blk(s), 0, 0))
    nxt = lambda blk: (lambda s: blk(jnp.minimum(s + 1, n_blk - 1)))
    fwd = [kspec(OP_KK, blk_f), kspec(OP_KK, nxt(blk_f))] + [kspec(o, blk_f) for o in (OP_R, OP_WF, OP_BF, OP_KF)]
    bwd = [kspec(OP_KK, blk_b), kspec(OP_KK, nxt(blk_b))] + [kspec(o, blk_b) for o in (OP_R, OP_WB, OP_BB, OP_KB)]
    return pl.pallas_call(
        _rwkv_scan_kernel,
        grid=(n_blk,),
        in_specs=fwd + bwd + [vspec(blk_f), vspec(blk_b)],
        out_specs=[vspec(blk_f), vspec(blk_b)],
        out_shape=[jax.ShapeDtypeStruct((tt, V_HALF, lanes), F32)] * 2,
        scratch_shapes=[pltpu.VMEM((2, n, V_HALF, lanes), F32), pltpu.VMEM((2, V_HALF, lanes), F32),
                        pltpu.VMEM((n, SCAN_ROWS + 8, lanes), F32), pltpu.VMEM((n, SCAN_ROWS + 8, lanes), F32)],
        compiler_params=_cparams(("arbitrary",)),
        name="rwkv_scan",
    )(*([ops_s] * 12), v_s, v_s)


def _merge_kernel(hf_ref, hb_ref, o_ref, y_ref, bonus_ref, g_ref, mg_ref, x_ref, mod_ref, gm_ref, lnw_ref,
                  lnb_ref, wpm_ref, wpr_ref, wout_ref, out_ref):
    c = RWKV_HEADS * RWKV_N
    hm = hf_ref[...] + hb_ref[...]
    ms = _group_sum(hm * hm, _block_diag_ones(MLSTM_DV)) * (1.0 / MLSTM_DV)
    hmn = hm * lax.rsqrt(ms + NORM_EPS) * gm_ref[...] * _sigmoid(o_ref[...])
    ym = jnp.dot(hmn.astype(BF16), wpm_ref[...], preferred_element_type=F32)
    bd = _block_diag_ones(RWKV_N)
    y = y_ref[...]
    yc = y - _group_sum(y, bd) * (1.0 / RWKV_N)
    var = _group_sum(yc * yc, bd) * (1.0 / RWKV_N)
    yn = yc * lax.rsqrt(var + GN_EPS) * lnw_ref[...] + lnb_ref[...]
    yr_in = ((yn + bonus_ref[...]) * g_ref[...]).astype(BF16)
    yr = jnp.dot(yr_in, wpr_ref[...], preferred_element_type=F32)
    mg = mg_ref[...]
    z = _sigmoid(mg[:, :c]) * ym + _sigmoid(mg[:, c:]) * yr
    yy = jnp.dot(z.astype(BF16), wout_ref[...], preferred_element_type=F32)
    out_ref[...] = x_ref[...] + mod_ref[GT1:GT1 + 1, :] * yy


def merge(hf, hb, qkvo, y, bonus, g, mg, x_all, modtab, gm, lnw, lnb, wpm, wpr, wout, nt_out):
    b, tt, d = x_all.shape
    ctx_tile = tt // ROW_TILE - 1
    row = lambda width, colblk=0: pl.BlockSpec((None, ROW_TILE, width), lambda bi, i: (bi, i, colblk))
    const = lambda shape: pl.BlockSpec(shape, lambda bi, i: (0,) * len(shape))
    return pl.pallas_call(
        _merge_kernel,
        grid=(b, nt_out),
        in_specs=[row(d), row(d), row(d, 2), row(d), row(d), row(d), row(2 * d), row(d),
                  pl.BlockSpec((None, None, MOD_ROWS, d), lambda bi, i: (bi, jnp.where(i == ctx_tile, 0, 1), 0, 0)),
                  const((1, d)), const((1, d)), const((1, d)), const((d, d)), const((d, d)), const((d, d))],
        out_specs=row(d),
        out_shape=jax.ShapeDtypeStruct((b, nt_out * ROW_TILE, d), F32),
        compiler_params=_cparams(("parallel", "parallel")),
        name="merge",
    )(hf, hb, qkvo, y, bonus, g, mg, x_all, modtab, gm, lnw, lnb, wpm, wpr, wout)


def _ffn_kernel(x_ref, g_ref, mod_ref, wg_ref, wu_ref, wd_ref, o_ref, h_scr, acc_scr):
    j = pl.program_id(2)

    @pl.when(j == 0)
    def _():
        h_scr[...] = _rms_mod(x_ref[...], g_ref[...], mod_ref[...], SH2, SC2).astype(BF16)
        acc_scr[...] = jnp.zeros_like(acc_scr)

    h = h_scr[...]
    a = jnp.dot(h, wg_ref[...], preferred_element_type=F32)
    u = jnp.dot(h, wu_ref[...], preferred_element_type=F32)
    t = (a * _sigmoid(a) * u).astype(BF16)
    acc_scr[...] += jnp.dot(t, wd_ref[...], preferred_element_type=F32)

    @pl.when(j == pl.num_programs(2) - 1)
    def _():
        o_ref[...] = x_ref[...] + mod_ref[GT2:GT2 + 1, :] * acc_scr[...]


def ffn_dense(x_all, g, modtab, wg, wu, wd, tf):
    b, tt, d = x_all.shape
    nt = tt // ROW_TILE
    ctx_tile = nt - 1
    f = wg.shape[1]
    row = pl.BlockSpec((None, ROW_TILE, d), lambda bi, i, j: (bi, i, 0))
    return pl.pallas_call(
        _ffn_kernel,
        grid=(b, nt, f // tf),
        in_specs=[row,
                  pl.BlockSpec((1, d), lambda bi, i, j: (0, 0)),
                  pl.BlockSpec((None, None, MOD_ROWS, d),
                               lambda bi, i, j: (bi, jnp.where(i == ctx_tile, 0, 1), 0, 0)),
                  pl.BlockSpec((d, tf), lambda bi, i, j: (0, j)),
                  pl.BlockSpec((d, tf), lambda bi, i, j: (0, j)),
                  pl.BlockSpec((tf, d), lambda bi, i, j: (j, 0))],
        out_specs=row,
        out_shape=jax.ShapeDtypeStruct((b, tt, d), F32),
        scratch_shapes=[pltpu.VMEM((ROW_TILE, d), BF16), pltpu.VMEM((ROW_TILE, d), F32)],
        compiler_params=_cparams(("parallel", "parallel", "arbitrary")),
        name="ffn_dense",
    )(x_all, g.reshape(1, d), modtab, wg, wu, wd)


MOE_ROWS = 512


def _moe_kernel(x_ref, g_ref, mod_ref, wr_ref, wg_ref, wu_ref, wd_ref, gfin_ref, o_ref, h_scr, acc_scr, gate_scr):
    e = pl.program_id(2)
    j = pl.program_id(3)
    first = jnp.logical_and(e == 0, j == 0)
    last = jnp.logical_and(e == pl.num_programs(2) - 1, j == pl.num_programs(3) - 1)

    @pl.when(first)
    def _():
        h = _rms_mod(x_ref[...], g_ref[...], mod_ref[...], SH2, SC2)
        h_scr[...] = h.astype(BF16)
        acc_scr[...] = jnp.zeros_like(acc_scr)
        logits = jnp.dot(h, wr_ref[...], precision=HIGHEST, preferred_element_type=F32)
        lane = lax.broadcasted_iota(jnp.int32, logits.shape, 1)
        logits = jnp.where(lane < N_EXPERTS, logits, -jnp.inf)
        m1 = jnp.max(logits, axis=-1, keepdims=True)
        i1 = jnp.min(jnp.where(logits == m1, lane, LANES), axis=-1, keepdims=True)
        rest = jnp.where(lane == i1, -jnp.inf, logits)
        m2 = jnp.max(rest, axis=-1, keepdims=True)
        i2 = jnp.min(jnp.where(rest == m2, lane, LANES), axis=-1, keepdims=True)
        e21 = jnp.exp(m2 - m1)
        p1 = 1.0 / (1.0 + e21)
        gate_scr[...] = jnp.where(lane == i1, p1, jnp.where(lane == i2, e21 * p1, 0.0))

    h = h_scr[...]
    a = jnp.dot(h, wg_ref[...], preferred_element_type=F32)
    u = jnp.dot(h, wu_ref[...], preferred_element_type=F32)
    lane = lax.broadcasted_iota(jnp.int32, gate_scr.shape, 1)
    gate = jnp.sum(jnp.where(lane == e, gate_scr[...], 0.0), axis=-1, keepdims=True)
    t = (a * _sigmoid(a) * u).astype(BF16)
    acc_scr[...] += gate * jnp.dot(t, wd_ref[...], preferred_element_type=F32)

    @pl.when(last)
    def _():
        xo = x_ref[...] + mod_ref[GT2:GT2 + 1, :] * acc_scr[...]
        ms = jnp.mean(xo * xo, axis=-1, keepdims=True)
        o_ref[...] = xo * lax.rsqrt(ms + NORM_EPS) * gfin_ref[...]


def moe_final(x_lat, g, modtab, w_router, wg, wu, wd, g_final, tf):
    b, s, d = x_lat.shape
    ne, _, f = wg.shape
    row = pl.BlockSpec((None, MOE_ROWS, d), lambda bi, i, e, j: (bi, i, 0))
    return pl.pallas_call(
        _moe_kernel,
        grid=(b, s // MOE_ROWS, ne, f // tf),
        in_specs=[row,
                  pl.BlockSpec((1, d), lambda bi, i, e, j: (0, 0)),
                  pl.BlockSpec((None, None, MOD_ROWS, d), lambda bi, i, e, j: (bi, 1, 0, 0)),
                  pl.BlockSpec((d, LANES), lambda bi, i, e, j: (0, 0)),
                  pl.BlockSpec((None, d, tf), lambda bi, i, e, j: (e, 0, j)),
                  pl.BlockSpec((None, d, tf), lambda bi, i, e, j: (e, 0, j)),
                  pl.BlockSpec((None, tf, d), lambda bi, i, e, j: (e, j, 0)),
                  pl.BlockSpec((1, d), lambda bi, i, e, j: (0, 0))],
        out_specs=row,
        out_shape=jax.ShapeDtypeStruct((b, s, d), F32),
        scratch_shapes=[pltpu.VMEM((MOE_ROWS, d), BF16), pltpu.VMEM((MOE_ROWS, d), F32),
                        pltpu.VMEM((MOE_ROWS, LANES), F32)],
        compiler_params=_cparams(("parallel", "parallel", "arbitrary", "arbitrary")),
        name="moe_final",
    )(x_lat, g.reshape(1, d), modtab, w_router, wg, wu, wd, g_final.reshape(1, d))


def kernel(x, c, ctx, c_ctx, w_ada, b_ada, g_norm_mix, g_norm_ffn, w_in, b_mlstm_gate, g_mlstm_norm, mu_shift, w0, w2, a0, a2, g2, k_k, k_a, r_k, ln_w, ln_b, w_proj_mlstm, w_proj_rwkv, w_out, w_ff_gate, w_ff_up, w_ff_down, w_router, w_exp_gate, w_exp_up, w_exp_down, g_final):
    b, seq, d = x.shape
    ctx_len = ctx.shape[1]
    depth = w_ada.shape[0]
    assert ctx_len == ROW_TILE and seq % ROW_TILE == 0 and seq % MOE_ROWS == 0 and 2 * b * RWKV_HEADS == LANES
    tt = seq + ctx_len
    m_all = b * tt
    H = MLSTM_HEADS
    cr = RWKV_HEADS * RWKV_N
    hq, hv = H * MLSTM_DQK, H * MLSTM_DV

    x_all = jnp.concatenate([x, ctx], axis=1)
    cond = jnp.concatenate([c, c_ctx[None, :], jnp.zeros((MOD_ROWS - b - 1, d), F32)], axis=0)
    mod_all = ada_mod(cond, w_ada, b_ada).reshape(depth, MOD_ROWS, 6, d)

    o_gate = 2 * hq + 2 * hv
    o_rwkv = o_gate + 4 * H
    rwkv_w = 3 * cr + 2 * DECAY_LORA + 2 * AAA_LORA + GATE_LORA
    o_mg = o_rwkv + rwkv_w

    out = None
    for l in range(depth):
        need_ctx = l < depth - 1
        pad = jnp.zeros((b, MOD_ROWS - 6, d), F32)
        mod_lat = jnp.concatenate([mod_all[l, :b], pad], axis=1)
        mod_ctx = jnp.broadcast_to(jnp.concatenate([mod_all[l, b], pad[0]], axis=0), (b, MOD_ROWS, d))
        modtab = jnp.stack([mod_ctx, mod_lat], axis=1)

        wl = w_in[l]
        w_qkvo = wl[:, :o_gate].astype(BF16)
        wgt = wl[:, o_gate:o_rwkv].reshape(d, 4, H)
        zpad = jnp.zeros((d, LANES - 2 * H), F32)
        w_gates = jnp.concatenate([wgt[:, 0], wgt[:, 2], zpad, wgt[:, 1], wgt[:, 3], zpad], axis=1)
        bg = b_mlstm_gate[l]
        lpad = jnp.zeros((LANES - 2 * H,), F32)
        bias_i = jnp.concatenate([bg[0], bg[2], lpad]).reshape(1, LANES)
        bias_f = jnp.concatenate([bg[1], bg[3], lpad]).reshape(1, LANES)
        w_rwkv = wl[:, o_rwkv:o_mg].astype(BF16)
        w_mg = wl[:, o_mg:].astype(BF16)

        hb16, hf32 = norm_mod(x_all, g_norm_mix[l], modtab)
        hb16 = hb16.reshape(m_all, d)
        qkvo = matmul(hb16, w_qkvo, 512, 1024).reshape(b, tt, o_gate)
        gates = matmul(hf32.reshape(m_all, d), w_gates, 512, 2 * LANES, precision=HIGHEST).reshape(b, tt, 2 * LANES)
        prw = matmul(hb16, w_rwkv, 512, rwkv_w // 3).reshape(b, tt, rwkv_w)
        mg = matmul(hb16, w_mg, 512, 1024).reshape(b, tt, 2 * d)

        h_f = mlstm_scan(qkvo, gates, bias_i, bias_f, rev=False)
        h_b = mlstm_scan(qkvo, gates, bias_i, bias_f, rev=True)

        zl = jnp.zeros((DECAY_LORA, cr), F32)
        w2cat = jnp.concatenate([jnp.concatenate([w2[l, 0], zl], axis=1),
                                 jnp.concatenate([zl, w2[l, 1]], axis=1)], axis=0).astype(BF16)
        a2cat = jnp.concatenate([jnp.concatenate([a2[l, 0], zl], axis=1),
                                 jnp.concatenate([zl, a2[l, 1]], axis=1)], axis=0).astype(BF16)
        ops_, v_, g_, bonus_ = rwkv_prepare(
            prw, mu_shift[l].reshape(1, rwkv_w), w0[l].reshape(1, 2 * cr), w2cat, a0[l].reshape(1, 2 * cr), a2cat,
            g2[l].astype(BF16), k_k[l].reshape(1, cr), k_a[l].reshape(1, cr), r_k[l].reshape(1, cr))
        y_f, y_b = rwkv_scan(to_scan_k(ops_), to_scan_v(v_), seq)
        y_ = from_scan(y_f, y_b, b)

        nt_out = tt // ROW_TILE if need_ctx else seq // ROW_TILE
        x_mid = merge(h_f, h_b, qkvo, y_, bonus_, g_, mg, x_all, modtab, g_mlstm_norm[l].reshape(1, hv),
                      ln_w[l].reshape(1, cr), ln_b[l].reshape(1, cr), w_proj_mlstm[l].astype(BF16),
                      w_proj_rwkv[l].astype(BF16), w_out[l].astype(BF16), nt_out)

        i = l // 2
        if l % 2 == 0:
            assert need_ctx
            x_all = ffn_dense(x_mid, g_norm_ffn[l], modtab, w_ff_gate[i].astype(BF16), w_ff_up[i].astype(BF16),
                              w_ff_down[i].astype(BF16), w_ff_gate.shape[2] // 2)
        else:
            assert not need_ctx
            wr = jnp.concatenate([w_router[i], jnp.zeros((d, LANES - N_EXPERTS), F32)], axis=1)
            out = moe_final(x_mid, g_norm_ffn[l], modtab, wr, w_exp_gate[i].astype(BF16), w_exp_up[i].astype(BF16),
                            w_exp_down[i].astype(BF16), g_final, w_exp_gate.shape[3] // 2)
    return out
```

```python
import functools

import jax
import jax.numpy as jnp
import numpy as np
from jax import lax
from jax.experimental import pallas as pl
from jax.experimental.pallas import tpu as pltpu

F32 = jnp.float32
BF16 = jnp.bfloat16
HIGHEST = lax.Precision.HIGHEST

GRID_W = 64
MLSTM_HEADS = 8
MLSTM_DQK = 64
MLSTM_DV = 128
GATE_SOFT_CAP = 15.0
RWKV_HEADS = 16
RWKV_N = 64
DECAY_LORA = 64
AAA_LORA = 64
GATE_LORA = 128
GN_EPS = 64e-5
N_EXPERTS = 8
NORM_EPS = 1e-6

LANES = 128
MXU_DIM = 256
ROW_TILE = 256
VMEM_LIMIT = 56 * 1024 * 1024

SH1, SC1, GT1, SH2, SC2, GT2 = range(6)
MOD_ROWS = 8


def _cparams(sem):
    return pltpu.CompilerParams(dimension_semantics=sem, vmem_limit_bytes=VMEM_LIMIT)


def _sigmoid(x):
    return 1.0 / (1.0 + jnp.exp(-x))


def _block_diag_ones(block):
    r = lax.broadcasted_iota(jnp.int32, (MXU_DIM, MXU_DIM), 0)
    c = lax.broadcasted_iota(jnp.int32, (MXU_DIM, MXU_DIM), 1)
    sh = int(np.log2(block))
    return jnp.where((r >> sh) == (c >> sh), 1.0, 0.0).astype(BF16)


def _group_sum(x, bd):
    outs = []
    for c in range(x.shape[1] // MXU_DIM):
        xs = x[:, c * MXU_DIM:(c + 1) * MXU_DIM]
        hi = xs.astype(BF16)
        r1 = xs - hi.astype(F32)
        mid = r1.astype(BF16)
        lo = (r1 - mid.astype(F32)).astype(BF16)
        acc = jnp.dot(hi, bd, preferred_element_type=F32)
        acc += jnp.dot(mid, bd, preferred_element_type=F32)
        acc += jnp.dot(lo, bd, preferred_element_type=F32)
        outs.append(acc)
    return jnp.concatenate(outs, axis=1)


def _rms_mod(x, g, mod, shift_row, scale_row):
    ms = jnp.mean(x * x, axis=-1, keepdims=True)
    y = x * lax.rsqrt(ms + NORM_EPS) * g
    return y * (1.0 + mod[scale_row:scale_row + 1, :]) + mod[shift_row:shift_row + 1, :]


def _ada_kernel(s_ref, w_ref, b_ref, o_ref):
    s = s_ref[...]
    s = s * _sigmoid(s)
    o_ref[...] = jnp.dot(s, w_ref[...], precision=HIGHEST, preferred_element_type=F32) + b_ref[...]


def ada_mod(cond, w_ada, b_ada):
    depth, d, n = w_ada.shape
    tn = n // 4
    return pl.pallas_call(
        _ada_kernel,
        grid=(depth, n // tn),
        in_specs=[pl.BlockSpec((MOD_ROWS, d), lambda l, j: (0, 0)),
                  pl.BlockSpec((None, d, tn), lambda l, j: (l, 0, j)),
                  pl.BlockSpec((None, 1, tn), lambda l, j: (l, 0, j))],
        out_specs=pl.BlockSpec((None, MOD_ROWS, tn), lambda l, j: (l, 0, j)),
        out_shape=jax.ShapeDtypeStruct((depth, MOD_ROWS, n), F32),
        compiler_params=_cparams(("arbitrary", "arbitrary")),
        name="ada_mod",
    )(cond, w_ada, b_ada.reshape(depth, 1, n))


def _norm_mod_kernel(x_ref, g_ref, mod_ref, hb_ref, hf_ref):
    h = _rms_mod(x_ref[...], g_ref[...], mod_ref[...], SH1, SC1)
    hb_ref[...] = h.astype(BF16)
    hf_ref[...] = h


def norm_mod(x_all, g, modtab):
    b, tt, d = x_all.shape
    nt = tt // ROW_TILE
    ctx_tile = nt - 1
    spec = pl.BlockSpec((None, ROW_TILE, d), lambda bi, i: (bi, i, 0))
    return pl.pallas_call(
        _norm_mod_kernel,
        grid=(b, nt),
        in_specs=[spec,
                  pl.BlockSpec((1, d), lambda bi, i: (0, 0)),
                  pl.BlockSpec((None, None, MOD_ROWS, d), lambda bi, i: (bi, jnp.where(i == ctx_tile, 0, 1), 0, 0))],
        out_specs=[spec, spec],
        out_shape=[jax.ShapeDtypeStruct((b, tt, d), BF16), jax.ShapeDtypeStruct((b, tt, d), F32)],
        compiler_params=_cparams(("parallel", "parallel")),
        name="norm_mod",
    )(x_all, g.reshape(1, d), modtab)


def _mm_kernel(a_ref, w_ref, o_ref, *, precision):
    o_ref[...] = jnp.dot(a_ref[...], w_ref[...], precision=precision,
                         preferred_element_type=F32).astype(o_ref.dtype)


def matmul(a, w, tm, tn, out_dtype=F32, precision=None):
    m, k = a.shape
    n = w.shape[1]
    return pl.pallas_call(
        functools.partial(_mm_kernel, precision=precision),
        grid=(n // tn, m // tm),
        in_specs=[pl.BlockSpec((tm, k), lambda j, i: (i, 0)),
                  pl.BlockSpec((k, tn), lambda j, i: (0, j))],
        out_specs=pl.BlockSpec((tm, tn), lambda j, i: (i, j)),
        out_shape=jax.ShapeDtypeStruct((m, n), out_dtype),
        compiler_params=_cparams(("parallel", "parallel")),
        name="matmul",
    )(a, w)


def _mlstm_kernel(qf, kf, vf, gif, gff, qb, kb, vb, gib, gfb, bi_ref, bf_ref, of, ob, ct_ref, m_ref):
    @pl.when(pl.program_id(1) == 0)
    def _():
        ct_ref[...] = jnp.zeros_like(ct_ref)
        m_ref[...] = jnp.zeros_like(m_ref)

    gates = (_mlstm_gates(gif, gff, bi_ref, bf_ref, False), _mlstm_gates(gib, gfb, bi_ref, bf_ref, True))
    refs = ((qf, kf, vf, of), (qb, kb, vb, ob))
    for h in range(MLSTM_HEADS):
        for d in range(2):
            _mlstm_head(h, *refs[d], ct_ref.at[d], m_ref.at[d], gates[d], bool(d))


def _mlstm_gates(gi_ref, gf_ref, bi_ref, bf_ref, rev):
    L = ROW_TILE

    def cap(a):
        return GATE_SOFT_CAP * jnp.tanh(a / GATE_SOFT_CAP)

    ic_all = cap(gi_ref[...] + bi_ref[...])
    fp = cap(gf_ref[...] + bf_ref[...])
    lf_all = jnp.minimum(fp, 0.0) - jnp.log(1.0 + jnp.exp(-jnp.abs(fp)))
    row = lax.broadcasted_iota(jnp.int32, (L, L), 0)
    col = lax.broadcasted_iota(jnp.int32, (L, L), 1)
    tri = (col >= row) if rev else (col <= row)
    b_all = jnp.dot(jnp.where(tri, 1.0, 0.0), lf_all, precision=HIGHEST, preferred_element_type=F32)
    src_t = (ic_all - b_all).T
    return ic_all, b_all, src_t, tri


def _mlstm_head(h, q_ref, k_ref, v_ref, o_ref, ct_ref, m_ref, gates, rev):
    L = ROW_TILE
    H, DQK, DV = MLSTM_HEADS, MLSTM_DQK, MLSTM_DV
    ic_all, b_all, src_t, tri = gates
    last = 0 if rev else L - 1
    lane0 = lax.broadcasted_iota(jnp.int32, (L, LANES), 1) == 0
    ones_blk = jnp.where(lane0, 1.0, 0.0).astype(BF16)
    scale = DQK ** -0.5
    ln = (H if rev else 0) + h
    bcol = b_all[:, ln:ln + 1]
    iccol = ic_all[:, ln:ln + 1]
    srow = src_t[ln:ln + 1, :]
    m_prev = m_ref[h][0:1, 0:1]
    dmat = jnp.where(tri, bcol + srow, -jnp.inf)
    inter = bcol + m_prev
    m_t = jnp.maximum(inter, jnp.max(dmat, axis=-1, keepdims=True))
    w_intra = jnp.exp(dmat - m_t)
    w_inter = jnp.exp(inter - m_t)
    qb = (q_ref[:, h * DQK:(h + 1) * DQK] * scale).astype(BF16)
    kf = k_ref[:, h * DQK:(h + 1) * DQK]
    kb = kf.astype(BF16)
    vaug = jnp.concatenate([v_ref[:, h * DV:(h + 1) * DV].astype(BF16), ones_blk], axis=1)
    sqk = lax.dot_general(qb, kb, (((1,), (1,)), ((), ())), preferred_element_type=F32)
    sw = (sqk * w_intra).astype(BF16)
    ct = ct_ref[h]
    num_aug = (jnp.dot(sw, vaug, preferred_element_type=F32)
               + w_inter * jnp.dot(qb, ct.astype(BF16), preferred_element_type=F32))
    num = num_aug[:, :DV]
    den = num_aug[:, DV:DV + 1]
    o_ref[:, h * DV:(h + 1) * DV] = num / jnp.maximum(jnp.abs(den), jnp.exp(-m_t))
    b_last = bcol[last:last + 1, :]
    w_src = b_last - bcol + iccol
    m_new = jnp.maximum(b_last + m_prev, jnp.max(w_src, axis=0, keepdims=True))
    a_src = jnp.exp(w_src - m_new)
    a_old = jnp.exp(b_last + m_prev - m_new)
    ks = (kf * a_src).astype(BF16)
    upd = lax.dot_general(ks, vaug, (((0,), (0,)), ((), ())), preferred_element_type=F32)
    ct_ref[h] = a_old * ct + upd
    m_ref[h] = jnp.broadcast_to(m_new, m_ref.shape[1:])


def mlstm_scan(qkvo, gates, bias_i, bias_f):
    b, tt, _ = qkvo.shape
    nt = tt // ROW_TILE
    ctx_tile = nt - 1
    hq = MLSTM_HEADS * MLSTM_DQK
    hv = MLSTM_HEADS * MLSTM_DV
    order_f = lambda s: jnp.where(s == 0, ctx_tile, s - 1)
    order_b = lambda s: jnp.where(s == 0, ctx_tile, ctx_tile - s)

    def operands(order):
        return [pl.BlockSpec((None, ROW_TILE, hq), lambda bi, s: (bi, order(s), 0)),
                pl.BlockSpec((None, ROW_TILE, hq), lambda bi, s: (bi, order(s), 1)),
                pl.BlockSpec((None, ROW_TILE, hv), lambda bi, s: (bi, order(s), 1)),
                pl.BlockSpec((None, ROW_TILE, LANES), lambda bi, s: (bi, order(s), 0)),
                pl.BlockSpec((None, ROW_TILE, LANES), lambda bi, s: (bi, order(s), 1))]

    bias = pl.BlockSpec((1, LANES), lambda bi, s: (0, 0))
    return pl.pallas_call(
        _mlstm_kernel,
        grid=(b, nt),
        in_specs=operands(order_f) + operands(order_b) + [bias, bias],
        out_specs=[pl.BlockSpec((None, ROW_TILE, hv), lambda bi, s: (bi, order_f(s), 0)),
                   pl.BlockSpec((None, ROW_TILE, hv), lambda bi, s: (bi, order_b(s), 0))],
        out_shape=[jax.ShapeDtypeStruct((b, tt, hv), F32)] * 2,
        scratch_shapes=[pltpu.VMEM((2, MLSTM_HEADS, MLSTM_DQK, 2 * MLSTM_DV), F32),
                        pltpu.VMEM((2, MLSTM_HEADS, 8, LANES), F32)],
        compiler_params=_cparams(("parallel", "arbitrary")),
        name="mlstm",
    )(*([qkvo] * 3 + [gates] * 2) * 2, bias_i, bias_f)


OP_KK, OP_R, OP_WF, OP_BF, OP_KF, OP_WB, OP_BB, OP_KB = range(8)
N_OPS = 8


def _rwkv_prep_kernel(p_ref, hp_ref, hn_ref, mu_ref, w0_ref, w2_ref, a0_ref, a2_ref, g2_ref, kk_ref, ka_ref,
                      rk_ref, ops_o, v_o, g_o, bonus_o, xs_ref):
    T = ROW_TILE
    i = pl.program_id(1)
    n_lat = pl.num_programs(1) - 1
    x = p_ref[...]
    W = x.shape[1]
    rowi = lax.broadcasted_iota(jnp.int32, (T, 1), 0)
    coli = lax.broadcasted_iota(jnp.int32, (1, W), 1)
    prev = pltpu.roll(x, 1, 0)
    nxt = pltpu.roll(x, T - 1, 0)

    @pl.when(i < n_lat)
    def _():
        gc = rowi & (GRID_W - 1)
        left = jnp.where(gc == 0, 0.0, prev)
        right = jnp.where(gc == GRID_W - 1, 0.0, nxt)
        hp = jnp.where(i == 0, 0.0, hp_ref[...])
        hn = jnp.where(i == n_lat - 1, 0.0, hn_ref[...])
        up = jnp.concatenate([hp, x[:T - GRID_W]], axis=0)
        down = jnp.concatenate([x[GRID_W:], hn], axis=0)
        q = W // 4
        xs_ref[...] = jnp.where(coli < q, left, jnp.where(coli < 2 * q, right, jnp.where(coli < 3 * q, up, down)))

    @pl.when(i == n_lat)
    def _():
        xs_ref[...] = jnp.where(coli < W // 2, jnp.where(rowi == 0, 0.0, prev), jnp.where(rowi == T - 1, 0.0, nxt))

    xm = x + (xs_ref[...] - x) * mu_ref[...]
    C = RWKV_HEADS * RWKV_N
    r = xm[:, 0:C]
    k = xm[:, C:2 * C]
    v = xm[:, 2 * C:3 * C]
    wd = jnp.tanh(xm[:, 3 * C:3 * C + 2 * DECAY_LORA]).astype(BF16)
    ad = xm[:, 3 * C + 2 * DECAY_LORA:3 * C + 2 * DECAY_LORA + 2 * AAA_LORA].astype(BF16)
    gd = _sigmoid(xm[:, 3 * C + 2 * DECAY_LORA + 2 * AAA_LORA:]).astype(BF16)
    bd = _block_diag_ones(RWKV_N)

    kk = k * kk_ref[...]
    nrm = jnp.sqrt(_group_sum(kk * kk, bd))
    kk = kk / jnp.maximum(nrm, 1e-12)
    u = w0_ref[...] + jnp.dot(wd, w2_ref[...], preferred_element_type=F32)
    decay = jnp.exp(-np.float32(np.exp(-0.5)) * _sigmoid(u))
    a = _sigmoid(a0_ref[...] + jnp.dot(ad, a2_ref[...], preferred_element_type=F32))
    ka = ka_ref[...]
    kd_f = k * (1.0 + (a[:, :C] - 1.0) * ka)
    kd_b = k * (1.0 + (a[:, C:] - 1.0) * ka)
    ops_o[OP_R] = r
    v_o[...] = v
    ops_o[OP_KK] = kk
    ops_o[OP_WF] = decay[:, :C]
    ops_o[OP_WB] = decay[:, C:]
    ops_o[OP_BF] = kk * a[:, :C]
    ops_o[OP_BB] = kk * a[:, C:]
    ops_o[OP_KF] = kd_f
    ops_o[OP_KB] = kd_b
    g_o[...] = jnp.dot(gd, g2_ref[...], preferred_element_type=F32)
    bonus_o[...] = _group_sum(r * (kd_f + kd_b) * rk_ref[...], bd) * v


def rwkv_prepare(p, mu, w0cat, w2cat, a0cat, a2cat, g2, k_k, k_a, r_k):
    b, tt, w = p.shape
    nt = tt // ROW_TILE
    n_lat = nt - 1
    hpt = ROW_TILE // GRID_W
    c = RWKV_HEADS * RWKV_N
    n_halo = tt // GRID_W
    const = lambda shape: pl.BlockSpec(shape, lambda bi, i: (0,) * len(shape))
    out_spec = pl.BlockSpec((None, ROW_TILE, c), lambda bi, i: (bi, i, 0))
    return pl.pallas_call(
        _rwkv_prep_kernel,
        grid=(b, nt),
        in_specs=[pl.BlockSpec((None, ROW_TILE, w), lambda bi, i: (bi, i, 0)),
                  pl.BlockSpec((None, GRID_W, w), lambda bi, i: (bi, jnp.maximum(i * hpt - 1, 0), 0)),
                  pl.BlockSpec((None, GRID_W, w), lambda bi, i: (bi, jnp.minimum(i * hpt + hpt, n_halo - 1), 0)),
                  const((1, w)), const((1, 2 * c)), const((2 * DECAY_LORA, 2 * c)), const((1, 2 * c)),
                  const((2 * AAA_LORA, 2 * c)), const((GATE_LORA, c)), const((1, c)), const((1, c)), const((1, c))],
        out_specs=[pl.BlockSpec((N_OPS, None, ROW_TILE, c), lambda bi, i: (0, bi, i, 0))] + [out_spec] * 3,
        out_shape=[jax.ShapeDtypeStruct((N_OPS, b, tt, c), F32)] + [jax.ShapeDtypeStruct((b, tt, c), F32)] * 3,
        scratch_shapes=[pltpu.VMEM((ROW_TILE, w), F32)],
        compiler_params=_cparams(("parallel", "parallel")),
        name="rwkv_prepare",
    )(p, p, p, mu, w0cat, w2cat, a0cat, a2cat, g2, k_k, k_a, r_k)


RELAYOUT_ROWS = 128
V_HALF = RWKV_N // 2


def _rows_to_heads(x_ref, zs_ref):
    for b in range(x_ref.shape[0]):
        for cb in range(x_ref.shape[2] // LANES):
            r0 = (b * (x_ref.shape[2] // LANES) + cb) * LANES
            zs_ref[r0:r0 + LANES, :] = x_ref[b, :, cb * LANES:(cb + 1) * LANES].T


def _to_scan_k_kernel(x_ref, o_ref, zs_ref):
    _rows_to_heads(x_ref, zs_ref)
    nbh = zs_ref.shape[0] // RWKV_N
    for k in range(RWKV_N):
        g = zs_ref[pl.ds(k, nbh, stride=RWKV_N), :]
        o_ref[k] = jnp.concatenate([g, g], axis=0).T


def to_scan_k(ops):
    n_ops, b, tt, c = ops.shape
    return pl.pallas_call(
        _to_scan_k_kernel,
        grid=(n_ops, tt // RELAYOUT_ROWS),
        in_specs=[pl.BlockSpec((None, b, RELAYOUT_ROWS, c), lambda o, i: (o, 0, i, 0))],
        out_specs=pl.BlockSpec((None, RWKV_N, RELAYOUT_ROWS, LANES), lambda o, i: (o, 0, i, 0)),
        out_shape=jax.ShapeDtypeStruct((n_ops, RWKV_N, tt, LANES), F32),
        scratch_shapes=[pltpu.VMEM((b * c, RELAYOUT_ROWS), F32)],
        compiler_params=_cparams(("parallel", "parallel")),
        name="to_scan_k",
    )(ops)


def _to_scan_v_kernel(x_ref, o_ref, zs_ref):
    _rows_to_heads(x_ref, zs_ref)
    nbh = zs_ref.shape[0] // RWKV_N
    for v in range(V_HALF):
        m = jnp.concatenate([zs_ref[pl.ds(v, nbh, stride=RWKV_N), :],
                             zs_ref[pl.ds(V_HALF + v, nbh, stride=RWKV_N), :]], axis=0)
        o_ref[:, v, :] = m.T


def to_scan_v(x):
    b, tt, c = x.shape
    return pl.pallas_call(
        _to_scan_v_kernel,
        grid=(tt // RELAYOUT_ROWS,),
        in_specs=[pl.BlockSpec((b, RELAYOUT_ROWS, c), lambda i: (0, i, 0))],
        out_specs=pl.BlockSpec((RELAYOUT_ROWS, V_HALF, LANES), lambda i: (i, 0, 0)),
        out_shape=jax.ShapeDtypeStruct((tt, V_HALF, LANES), F32),
        scratch_shapes=[pltpu.VMEM((b * c, RELAYOUT_ROWS), F32)],
        compiler_params=_cparams(("parallel",)),
        name="to_scan_v",
    )(x)


def _from_scan_kernel(yf_ref, yb_ref, o_ref, zs_ref):
    nbh = zs_ref.shape[0] // RWKV_N
    for v in range(V_HALF):
        t = (yf_ref[:, v, :] + yb_ref[:, v, :]).T
        zs_ref[pl.ds(v, nbh, stride=RWKV_N), :] = t[:nbh]
        zs_ref[pl.ds(V_HALF + v, nbh, stride=RWKV_N), :] = t[nbh:]
    for b in range(o_ref.shape[0]):
        for cb in range(o_ref.shape[2] // LANES):
            r0 = (b * (o_ref.shape[2] // LANES) + cb) * LANES
            o_ref[b, :, cb * LANES:(cb + 1) * LANES] = zs_ref[r0:r0 + LANES, :].T


def from_scan(yf, yb, b):
    tt = yf.shape[0]
    c = RWKV_HEADS * RWKV_N
    spec = pl.BlockSpec((RELAYOUT_ROWS, V_HALF, LANES), lambda i: (i, 0, 0))
    return pl.pallas_call(
        _from_scan_kernel,
        grid=(tt // RELAYOUT_ROWS,),
        in_specs=[spec, spec],
        out_specs=pl.BlockSpec((b, RELAYOUT_ROWS, c), lambda i: (0, i, 0)),
        out_shape=jax.ShapeDtypeStruct((b, tt, c), F32),
        scratch_shapes=[pltpu.VMEM((b * c, RELAYOUT_ROWS), F32)],
        compiler_params=_cparams(("parallel",)),
        name="from_scan",
    )(yf, yb)


SCAN_ROWS = 32


def _rwkv_scan_kernel(kk_f, kkn_f, r_f, w_f, b_f, k_f, kk_b, kkn_b, r_b, w_b, b_b, k_b, v_f, v_b, y_f, y_b,
                      s_ref, sa_ref, kkx_f, kkx_b):
    N, TB, PAD = RWKV_N, SCAN_ROWS, 8

    @pl.when(pl.program_id(0) == 0)
    def _():
        s_ref[...] = jnp.zeros_like(s_ref)
        sa_ref[...] = jnp.zeros_like(sa_ref)

    kkx_f[:, 0:TB, :] = kk_f[...]
    kkx_f[:, TB:TB + PAD, :] = kkn_f[:, 0:PAD, :]
    kkx_b[:, PAD:PAD + TB, :] = kk_b[...]
    kkx_b[:, 0:PAD, :] = kkn_b[:, TB - PAD:TB, :]

    dirs = ((kkx_f, r_f, w_f, b_f, k_f, v_f, y_f), (kkx_b, r_b, w_b, b_b, k_b, v_b, y_b))

    def step(j, carry):
        for d, (kkx_r, r_r, w_r, b_r, k_r, v_r, y_r) in enumerate(dirs):
            row, nxt = (j, j + 1) if d == 0 else (TB - 1 - j, PAD + TB - 2 - j)
            op = lambda ref, kx, rw: ref[kx, pl.ds(rw, 1), :]
            sa = sa_ref[d]
            vt = v_r[row]
            acc_sa = [jnp.zeros((V_HALF, LANES), F32), jnp.zeros((V_HALF, LANES), F32)]
            acc_y = [jnp.zeros((V_HALF, LANES), F32), jnp.zeros((V_HALF, LANES), F32)]
            for kx in range(N):
                sk = s_ref[d, kx] * op(w_r, kx, row) - sa * op(b_r, kx, row) + vt * op(k_r, kx, row)
                s_ref[d, kx] = sk
                acc_sa[kx % 2] = acc_sa[kx % 2] + sk * op(kkx_r, kx, nxt)
                acc_y[kx % 2] = acc_y[kx % 2] + sk * op(r_r, kx, row)
            sa_ref[d] = acc_sa[0] + acc_sa[1]
            y_r[row] = acc_y[0] + acc_y[1]
        return carry

    lax.fori_loop(0, SCAN_ROWS, step, 0)


def rwkv_scan(ops_s, v_s, seq):
    _, n, tt, lanes = ops_s.shape
    n_lat = seq // SCAN_ROWS
    n_blk = tt // SCAN_ROWS
    n_ctx = n_blk - n_lat
    blk_f = lambda s: jnp.where(s < n_ctx, n_lat + s, s - n_ctx)
    blk_b = lambda s: n_blk - 1 - s
    kspec = lambda o, blk: pl.BlockSpec((None, n, SCAN_ROWS, lanes), lambda s: (o, 0, blk(s), 0))
    vspec = lambda blk: pl.BlockSpec((SCAN_ROWS, V_HALF, lanes), lambda s: (blk(s), 0, 0))
    nxt = lambda blk: (lambda s: blk(jnp.minimum(s + 1, n_blk - 1)))
    fwd = [kspec(OP_KK, blk_f), kspec(OP_KK, nxt(blk_f))] + [kspec(o, blk_f) for o in (OP_R, OP_WF, OP_BF, OP_KF)]
    bwd = [kspec(OP_KK, blk_b), kspec(OP_KK, nxt(blk_b))] + [kspec(o, blk_b) for o in (OP_R, OP_WB, OP_BB, OP_KB)]
    return pl.pallas_call(
        _rwkv_scan_kernel,
        grid=(n_blk,),
        in_specs=fwd + bwd + [vspec(blk_f), vspec(blk_b)],
        out_specs=[vspec(blk_f), vspec(blk_b)],
        out_shape=[jax.ShapeDtypeStruct((tt, V_HALF, lanes), F32)] * 2,
        scratch_shapes=[pltpu.VMEM((2, n, V_HALF, lanes), F32), pltpu.VMEM((2, V_HALF, lanes), F32),
                        pltpu.VMEM((n, SCAN_ROWS + 8, lanes), F32), pltpu.VMEM((n, SCAN_ROWS + 8, lanes), F32)],
        compiler_params=_cparams(("arbitrary",)),
        name="rwkv_scan",
    )(*([ops_s] * 12), v_s, v_s)


def _merge_kernel(hf_ref, hb_ref, o_ref, y_ref, bonus_ref, g_ref, mg_ref, x_ref, mod_ref, gm_ref, lnw_ref,
                  lnb_ref, wpm_ref, wpr_ref, wout_ref, out_ref):
    c = RWKV_HEADS * RWKV_N
    hm = hf_ref[...] + hb_ref[...]
    ms = _group_sum(hm * hm, _block_diag_ones(MLSTM_DV)) * (1.0 / MLSTM_DV)
    hmn = hm * lax.rsqrt(ms + NORM_EPS) * gm_ref[...] * _sigmoid(o_ref[...])
    ym = jnp.dot(hmn.astype(BF16), wpm_ref[...], preferred_element_type=F32)
    bd = _block_diag_ones(RWKV_N)
    y = y_ref[...]
    yc = y - _group_sum(y, bd) * (1.0 / RWKV_N)
    var = _group_sum(yc * yc, bd) * (1.0 / RWKV_N)
    yn = yc * lax.rsqrt(var + GN_EPS) * lnw_ref[...] + lnb_ref[...]
    yr_in = ((yn + bonus_ref[...]) * g_ref[...]).astype(BF16)
    yr = jnp.dot(yr_in, wpr_ref[...], preferred_element_type=F32)
    mg = mg_ref[...]
    z = _sigmoid(mg[:, :c]) * ym + _sigmoid(mg[:, c:]) * yr
    yy = jnp.dot(z.astype(BF16), wout_ref[...], preferred_element_type=F32)
    out_ref[...] = x_ref[...] + mod_ref[GT1:GT1 + 1, :] * yy


def merge(hf, hb, qkvo, y, bonus, g, mg, x_all, modtab, gm, lnw, lnb, wpm, wpr, wout, nt_out):
    b, tt, d = x_all.shape
    ctx_tile = tt // ROW_TILE - 1
    row = lambda width, colblk=0: pl.BlockSpec((None, ROW_TILE, width), lambda bi, i: (bi, i, colblk))
    const = lambda shape: pl.BlockSpec(shape, lambda bi, i: (0,) * len(shape))
    return pl.pallas_call(
        _merge_kernel,
        grid=(b, nt_out),
        in_specs=[row(d), row(d), row(d, 2), row(d), row(d), row(d), row(2 * d), row(d),
                  pl.BlockSpec((None, None, MOD_ROWS, d), lambda bi, i: (bi, jnp.where(i == ctx_tile, 0, 1), 0, 0)),
                  const((1, d)), const((1, d)), const((1, d)), const((d, d)), const((d, d)), const((d, d))],
        out_specs=row(d),
        out_shape=jax.ShapeDtypeStruct((b, nt_out * ROW_TILE, d), F32),
        compiler_params=_cparams(("parallel", "parallel")),
        name="merge",
    )(hf, hb, qkvo, y, bonus, g, mg, x_all, modtab, gm, lnw, lnb, wpm, wpr, wout)


def _ffn_kernel(x_ref, g_ref, mod_ref, wg_ref, wu_ref, wd_ref, o_ref, h_scr, acc_scr):
    j = pl.program_id(2)

    @pl.when(j == 0)
    def _():
        h_scr[...] = _rms_mod(x_ref[...], g_ref[...], mod_ref[...], SH2, SC2).astype(BF16)
        acc_scr[...] = jnp.zeros_like(acc_scr)

    h = h_scr[...]
    a = jnp.dot(h, wg_ref[...], preferred_element_type=F32)
    u = jnp.dot(h, wu_ref[...], preferred_element_type=F32)
    t = (a * _sigmoid(a) * u).astype(BF16)
    acc_scr[...] += jnp.dot(t, wd_ref[...], preferred_element_type=F32)

    @pl.when(j == pl.num_programs(2) - 1)
    def _():
        o_ref[...] = x_ref[...] + mod_ref[GT2:GT2 + 1, :] * acc_scr[...]


def ffn_dense(x_all, g, modtab, wg, wu, wd, tf):
    b, tt, d = x_all.shape
    nt = tt // ROW_TILE
    ctx_tile = nt - 1
    f = wg.shape[1]
    row = pl.BlockSpec((None, ROW_TILE, d), lambda bi, i, j: (bi, i, 0))
    return pl.pallas_call(
        _ffn_kernel,
        grid=(b, nt, f // tf),
        in_specs=[row,
                  pl.BlockSpec((1, d), lambda bi, i, j: (0, 0)),
                  pl.BlockSpec((None, None, MOD_ROWS, d),
                               lambda bi, i, j: (bi, jnp.where(i == ctx_tile, 0, 1), 0, 0)),
                  pl.BlockSpec((d, tf), lambda bi, i, j: (0, j)),
                  pl.BlockSpec((d, tf), lambda bi, i, j: (0, j)),
                  pl.BlockSpec((tf, d), lambda bi, i, j: (j, 0))],
        out_specs=row,
        out_shape=jax.ShapeDtypeStruct((b, tt, d), F32),
        scratch_shapes=[pltpu.VMEM((ROW_TILE, d), BF16), pltpu.VMEM((ROW_TILE, d), F32)],
        compiler_params=_cparams(("parallel", "parallel", "arbitrary")),
        name="ffn_dense",
    )(x_all, g.reshape(1, d), modtab, wg, wu, wd)


MOE_ROWS = 512


TOP_K = 2
MOE_TOKENS = 256
SEL_E1, SEL_E2, SEL_P1, SEL_P2 = range(4)


def _moe_route_kernel(x_ref, g_ref, mod_ref, wr_ref, h_o, sel_o):
    h = _rms_mod(x_ref[...], g_ref[...], mod_ref[...], SH2, SC2)
    h_o[...] = h
    logits = jnp.dot(h, wr_ref[...], precision=HIGHEST, preferred_element_type=F32)
    lane = lax.broadcasted_iota(jnp.int32, logits.shape, 1)
    logits = jnp.where(lane < N_EXPERTS, logits, -jnp.inf)
    m1 = jnp.max(logits, axis=-1, keepdims=True)
    i1 = jnp.min(jnp.where(logits == m1, lane, LANES), axis=-1, keepdims=True)
    rest = jnp.where(lane == i1, -jnp.inf, logits)
    m2 = jnp.max(rest, axis=-1, keepdims=True)
    i2 = jnp.min(jnp.where(rest == m2, lane, LANES), axis=-1, keepdims=True)
    e21 = jnp.exp(m2 - m1)
    p1 = 1.0 / (1.0 + e21)
    sel_o[...] = jnp.where(lane == SEL_E1, i1.astype(F32),
                           jnp.where(lane == SEL_E2, i2.astype(F32),
                                     jnp.where(lane == SEL_P1, p1, jnp.where(lane == SEL_P2, e21 * p1, 0.0))))


def moe_route(x_lat, g, modtab, w_router):
    b, s, d = x_lat.shape
    nt = s // MOE_ROWS
    return pl.pallas_call(
        _moe_route_kernel,
        grid=(b, nt),
        in_specs=[pl.BlockSpec((None, MOE_ROWS, d), lambda bi, i: (bi, i, 0)),
                  pl.BlockSpec((1, d), lambda bi, i: (0, 0)),
                  pl.BlockSpec((None, None, MOD_ROWS, d), lambda bi, i: (bi, 1, 0, 0)),
                  pl.BlockSpec((d, LANES), lambda bi, i: (0, 0))],
        out_specs=[pl.BlockSpec((MOE_ROWS, d), lambda bi, i: (bi * nt + i, 0)),
                   pl.BlockSpec((MOE_ROWS, LANES), lambda bi, i: (bi * nt + i, 0))],
        out_shape=[jax.ShapeDtypeStruct((b * s, d), F32), jax.ShapeDtypeStruct((b * s, LANES), F32)],
        compiler_params=_cparams(("parallel", "parallel")),
        name="moe_route",
    )(x_lat, g.reshape(1, d), modtab, w_router)


def _row_copy(src, src_row, dst, dst_row, sem):
    return pltpu.make_async_copy(src.at[pl.ds(src_row, 1)], dst.at[pl.ds(dst_row, 1)], sem)


def _moe_dispatch_kernel(pos_ref, h_hbm, xg_in, xg_out, sem):
    del xg_in
    n = pos_ref.shape[-1]
    t0 = pl.program_id(0) * (n // TOP_K)

    def start(a, c):
        _row_copy(h_hbm, t0 + lax.shift_right_logical(a, 1), xg_out, pos_ref[0, a], sem).start()
        return c

    def wait(a, c):
        _row_copy(h_hbm, 0, xg_out, 0, sem).wait()
        return c

    lax.fori_loop(0, n, start, 0, unroll=8)
    lax.fori_loop(0, n, wait, 0, unroll=8)


def moe_dispatch(pos, h, rows):
    m, d = h.shape
    n = TOP_K * MOE_TOKENS
    return pl.pallas_call(
        _moe_dispatch_kernel,
        grid=(m // MOE_TOKENS,),
        in_specs=[pl.BlockSpec((None, 1, n), lambda i: (i, 0, 0), memory_space=pltpu.SMEM),
                  pl.BlockSpec(memory_space=pl.ANY),
                  pl.BlockSpec(memory_space=pl.ANY)],
        out_specs=pl.BlockSpec(memory_space=pl.ANY),
        out_shape=jax.ShapeDtypeStruct((rows, d), F32),
        scratch_shapes=[pltpu.SemaphoreType.DMA(())],
        input_output_aliases={2: 0},
        compiler_params=_cparams(("arbitrary",)),
        name="moe_dispatch",
    )(pos.reshape(m // MOE_TOKENS, 1, n), h, jnp.zeros((rows, d), F32))


def _moe_experts_kernel(te_ref, nu_ref, xg_ref, wg_ref, wu_ref, wd_ref, y_ref, h_scr, acc_scr):
    del te_ref
    i = pl.program_id(0)
    j = pl.program_id(1)
    used = i < nu_ref[0]
    last = j == pl.num_programs(1) - 1

    @pl.when(jnp.logical_and(used, j == 0))
    def _():
        h_scr[...] = xg_ref[...].astype(BF16)
        acc_scr[...] = jnp.zeros_like(acc_scr)

    @pl.when(used)
    def _():
        h = h_scr[...]
        a = jnp.dot(h, wg_ref[...], preferred_element_type=F32)
        u = jnp.dot(h, wu_ref[...], preferred_element_type=F32)
        t = (a * _sigmoid(a) * u).astype(BF16)
        acc_scr[...] += jnp.dot(t, wd_ref[...], preferred_element_type=F32)

    @pl.when(jnp.logical_and(used, last))
    def _():
        y_ref[...] = acc_scr[...]

    @pl.when(jnp.logical_and(jnp.logical_not(used), last))
    def _():
        y_ref[...] = jnp.zeros_like(y_ref)


def moe_experts(tile_expert, n_used, xg, wg, wu, wd, tf):
    rows, d = xg.shape
    f = wg.shape[2]
    return pl.pallas_call(
        _moe_experts_kernel,
        grid_spec=pltpu.PrefetchScalarGridSpec(
            num_scalar_prefetch=2,
            grid=(rows // MOE_ROWS, f // tf),
            in_specs=[pl.BlockSpec((MOE_ROWS, d), lambda i, j, te, nu: (i, 0)),
                      pl.BlockSpec((None, d, tf), lambda i, j, te, nu: (te[i], 0, j)),
                      pl.BlockSpec((None, d, tf), lambda i, j, te, nu: (te[i], 0, j)),
                      pl.BlockSpec((None, tf, d), lambda i, j, te, nu: (te[i], j, 0))],
            out_specs=pl.BlockSpec((MOE_ROWS, d), lambda i, j, te, nu: (i, 0)),
            scratch_shapes=[pltpu.VMEM((MOE_ROWS, d), BF16), pltpu.VMEM((MOE_ROWS, d), F32)]),
        out_shape=jax.ShapeDtypeStruct((rows, d), F32),
        compiler_params=_cparams(("arbitrary", "arbitrary")),
        name="moe_experts",
    )(tile_expert, n_used, xg, wg, wu, wd)


def _moe_combine_kernel(pos_ref, sel_ref, x_ref, mod_ref, gfin_ref, y_hbm, o_ref, ybuf, sem):
    n = pos_ref.shape[-1]

    def start(a, c):
        _row_copy(y_hbm, pos_ref[0, a], ybuf.at[a & 1], lax.shift_right_logical(a, 1), sem).start()
        return c

    def wait(a, c):
        _row_copy(y_hbm, 0, ybuf.at[0], 0, sem).wait()
        return c

    lax.fori_loop(0, n, start, 0, unroll=8)
    lax.fori_loop(0, n, wait, 0, unroll=8)
    sel = sel_ref[...]
    yy = sel[:, SEL_P1:SEL_P1 + 1] * ybuf[0] + sel[:, SEL_P2:SEL_P2 + 1] * ybuf[1]
    xo = x_ref[...] + mod_ref[GT2:GT2 + 1, :] * yy
    ms = jnp.mean(xo * xo, axis=-1, keepdims=True)
    o_ref[...] = xo * lax.rsqrt(ms + NORM_EPS) * gfin_ref[...]


def moe_combine(pos, sel, x_lat, modtab, g_final, y):
    b, s, d = x_lat.shape
    nt = s // MOE_TOKENS
    n = TOP_K * MOE_TOKENS
    row = pl.BlockSpec((None, MOE_TOKENS, d), lambda bi, i: (bi, i, 0))
    return pl.pallas_call(
        _moe_combine_kernel,
        grid=(b, nt),
        in_specs=[pl.BlockSpec((None, 1, n), lambda bi, i: (bi * nt + i, 0, 0), memory_space=pltpu.SMEM),
                  pl.BlockSpec((MOE_TOKENS, LANES), lambda bi, i: (bi * nt + i, 0)),
                  row,
                  pl.BlockSpec((None, None, MOD_ROWS, d), lambda bi, i: (bi, 1, 0, 0)),
                  pl.BlockSpec((1, d), lambda bi, i: (0, 0)),
                  pl.BlockSpec(memory_space=pl.ANY)],
        out_specs=row,
        out_shape=jax.ShapeDtypeStruct((b, s, d), F32),
        scratch_shapes=[pltpu.VMEM((TOP_K, MOE_TOKENS, d), F32), pltpu.SemaphoreType.DMA(())],
        compiler_params=_cparams(("arbitrary", "arbitrary")),
        name="moe_combine",
    )(pos.reshape(b * nt, 1, n), sel, x_lat, modtab, g_final.reshape(1, d), y)


def moe_final(x_lat, g, modtab, w_router, wg, wu, wd, g_final, tf):
    b, s, d = x_lat.shape
    m = b * s
    ne = wg.shape[0]
    h, sel = moe_route(x_lat, g, modtab, w_router)
    e = sel[:, SEL_E1:SEL_E2 + 1].astype(jnp.int32).reshape(m * TOP_K)
    onehot = (e[:, None] == jnp.arange(ne, dtype=jnp.int32)[None, :]).astype(jnp.int32)
    running = jnp.cumsum(onehot, axis=0)
    counts = running[-1]
    padded = (counts + MOE_ROWS - 1) // MOE_ROWS * MOE_ROWS
    group_end = jnp.cumsum(padded)
    group_start = group_end - padded
    pos = jnp.sum(onehot * (group_start[None, :] + running - 1), axis=1).astype(jnp.int32)
    n_tiles = m * TOP_K // MOE_ROWS + ne
    tile_expert = jnp.minimum(
        jnp.sum(jnp.arange(n_tiles, dtype=jnp.int32)[:, None] >= (group_end // MOE_ROWS)[None, :], axis=1),
        ne - 1).astype(jnp.int32)
    n_used = (group_end[-1:] // MOE_ROWS).astype(jnp.int32)
    xg = moe_dispatch(pos, h, n_tiles * MOE_ROWS)
    y = moe_experts(tile_expert, n_used, xg, wg, wu, wd, tf)
    return moe_combine(pos, sel, x_lat, modtab, g_final, y)


def kernel(x, c, ctx, c_ctx, w_ada, b_ada, g_norm_mix, g_norm_ffn, w_in, b_mlstm_gate, g_mlstm_norm, mu_shift, w0, w2, a0, a2, g2, k_k, k_a, r_k, ln_w, ln_b, w_proj_mlstm, w_proj_rwkv, w_out, w_ff_gate, w_ff_up, w_ff_down, w_router, w_exp_gate, w_exp_up, w_exp_down, g_final):
    b, seq, d = x.shape
    ctx_len = ctx.shape[1]
    depth = w_ada.shape[0]
    assert ctx_len == ROW_TILE and seq % ROW_TILE == 0 and seq % MOE_ROWS == 0 and 2 * b * RWKV_HEADS == LANES
    tt = seq + ctx_len
    m_all = b * tt
    H = MLSTM_HEADS
    cr = RWKV_HEADS * RWKV_N
    hq, hv = H * MLSTM_DQK, H * MLSTM_DV

    x_all = jnp.concatenate([x, ctx], axis=1)
    cond = jnp.concatenate([c, c_ctx[None, :], jnp.zeros((MOD_ROWS - b - 1, d), F32)], axis=0)
    mod_all = ada_mod(cond, w_ada, b_ada).reshape(depth, MOD_ROWS, 6, d)

    o_gate = 2 * hq + 2 * hv
    o_rwkv = o_gate + 4 * H
    rwkv_w = 3 * cr + 2 * DECAY_LORA + 2 * AAA_LORA + GATE_LORA
    o_mg = o_rwkv + rwkv_w

    out = None
    for l in range(depth):
        need_ctx = l < depth - 1
        pad = jnp.zeros((b, MOD_ROWS - 6, d), F32)
        mod_lat = jnp.concatenate([mod_all[l, :b], pad], axis=1)
        mod_ctx = jnp.broadcast_to(jnp.concatenate([mod_all[l, b], pad[0]], axis=0), (b, MOD_ROWS, d))
        modtab = jnp.stack([mod_ctx, mod_lat], axis=1)

        wl = w_in[l]
        w_qkvo = wl[:, :o_gate].astype(BF16)
        wgt = wl[:, o_gate:o_rwkv].reshape(d, 4, H)
        zpad = jnp.zeros((d, LANES - 2 * H), F32)
        w_gates = jnp.concatenate([wgt[:, 0], wgt[:, 2], zpad, wgt[:, 1], wgt[:, 3], zpad], axis=1)
        bg = b_mlstm_gate[l]
        lpad = jnp.zeros((LANES - 2 * H,), F32)
        bias_i = jnp.concatenate([bg[0], bg[2], lpad]).reshape(1, LANES)
        bias_f = jnp.concatenate([bg[1], bg[3], lpad]).reshape(1, LANES)
        w_rwkv = wl[:, o_rwkv:o_mg].astype(BF16)
        w_mg = wl[:, o_mg:].astype(BF16)

        hb16, hf32 = norm_mod(x_all, g_norm_mix[l], modtab)
        hb16 = hb16.reshape(m_all, d)
        qkvo = matmul(hb16, w_qkvo, 512, 1024).reshape(b, tt, o_gate)
        gates = matmul(hf32.reshape(m_all, d), w_gates, 512, 2 * LANES, precision=HIGHEST).reshape(b, tt, 2 * LANES)
        prw = matmul(hb16, w_rwkv, 512, rwkv_w // 3).reshape(b, tt, rwkv_w)
        mg = matmul(hb16, w_mg, 512, 1024).reshape(b, tt, 2 * d)

        h_f, h_b = mlstm_scan(qkvo, gates, bias_i, bias_f)

        zl = jnp.zeros((DECAY_LORA, cr), F32)
        w2cat = jnp.concatenate([jnp.concatenate([w2[l, 0], zl], axis=1),
                                 jnp.concatenate([zl, w2[l, 1]], axis=1)], axis=0).astype(BF16)
        a2cat = jnp.concatenate([jnp.concatenate([a2[l, 0], zl], axis=1),
                                 jnp.concatenate([zl, a2[l, 1]], axis=1)], axis=0).astype(BF16)
        ops_, v_, g_, bonus_ = rwkv_prepare(
            prw, mu_shift[l].reshape(1, rwkv_w), w0[l].reshape(1, 2 * cr), w2cat, a0[l].reshape(1, 2 * cr), a2cat,
            g2[l].astype(BF16), k_k[l].reshape(1, cr), k_a[l].reshape(1, cr), r_k[l].reshape(1, cr))
        y_f, y_b = rwkv_scan(to_scan_k(ops_), to_scan_v(v_), seq)
        y_ = from_scan(y_f, y_b, b)

        nt_out = tt // ROW_TILE if need_ctx else seq // ROW_TILE
        x_mid = merge(h_f, h_b, qkvo, y_, bonus_, g_, mg, x_all, modtab, g_mlstm_norm[l].reshape(1, hv),
                      ln_w[l].reshape(1, cr), ln_b[l].reshape(1, cr), w_proj_mlstm[l].astype(BF16),
                      w_proj_rwkv[l].astype(BF16), w_out[l].astype(BF16), nt_out)

        i = l // 2
        if l % 2 == 0:
            assert need_ctx
            x_all = ffn_dense(x_mid, g_norm_ffn[l], modtab, w_ff_gate[i].astype(BF16), w_ff_up[i].astype(BF16),
                              w_ff_down[i].astype(BF16), w_ff_gate.shape[2] // 2)
        else:
            assert not need_ctx
            wr = jnp.concatenate([w_router[i], jnp.zeros((d, LANES - N_EXPERTS), F32)], axis=1)
            out = moe_final(x_mid, g_norm_ffn[l], modtab, wr, w_exp_gate[i].astype(BF16), w_exp_up[i].astype(BF16),
                            w_exp_down[i].astype(BF16), g_final, w_exp_gate.shape[3] // 2)
    return out
```

```python
import functools

import jax
import jax.numpy as jnp
import numpy as np
from jax import lax
from jax.experimental import pallas as pl
from jax.experimental.pallas import tpu as pltpu

F32 = jnp.float32
BF16 = jnp.bfloat16
HIGHEST = lax.Precision.HIGHEST

GRID_W = 64
MLSTM_HEADS = 8
MLSTM_DQK = 64
MLSTM_DV = 128
GATE_SOFT_CAP = 15.0
RWKV_HEADS = 16
RWKV_N = 64
DECAY_LORA = 64
AAA_LORA = 64
GATE_LORA = 128
GN_EPS = 64e-5
N_EXPERTS = 8
NORM_EPS = 1e-6

LANES = 128
MXU_DIM = 256
ROW_TILE = 256
VMEM_LIMIT = 56 * 1024 * 1024

SH1, SC1, GT1, SH2, SC2, GT2 = range(6)
MOD_ROWS = 8


def _cparams(sem):
    return pltpu.CompilerParams(dimension_semantics=sem, vmem_limit_bytes=VMEM_LIMIT)


def _sigmoid(x):
    return 1.0 / (1.0 + jnp.exp(-x))


def _block_diag_ones(block):
    r = lax.broadcasted_iota(jnp.int32, (MXU_DIM, MXU_DIM), 0)
    c = lax.broadcasted_iota(jnp.int32, (MXU_DIM, MXU_DIM), 1)
    sh = int(np.log2(block))
    return jnp.where((r >> sh) == (c >> sh), 1.0, 0.0).astype(BF16)


def _group_sum(x, bd):
    outs = []
    for c in range(x.shape[1] // MXU_DIM):
        xs = x[:, c * MXU_DIM:(c + 1) * MXU_DIM]
        hi = xs.astype(BF16)
        r1 = xs - hi.astype(F32)
        mid = r1.astype(BF16)
        lo = (r1 - mid.astype(F32)).astype(BF16)
        acc = jnp.dot(hi, bd, preferred_element_type=F32)
        acc += jnp.dot(mid, bd, preferred_element_type=F32)
        acc += jnp.dot(lo, bd, preferred_element_type=F32)
        outs.append(acc)
    return jnp.concatenate(outs, axis=1)


def _rms_mod(x, g, mod, shift_row, scale_row):
    ms = jnp.mean(x * x, axis=-1, keepdims=True)
    y = x * lax.rsqrt(ms + NORM_EPS) * g
    return y * (1.0 + mod[scale_row:scale_row + 1, :]) + mod[shift_row:shift_row + 1, :]


def _ada_kernel(s_ref, w_ref, b_ref, o_ref):
    s = s_ref[...]
    s = s * _sigmoid(s)
    o_ref[...] = jnp.dot(s, w_ref[...], precision=HIGHEST, preferred_element_type=F32) + b_ref[...]


def ada_mod(cond, w_ada, b_ada):
    depth, d, n = w_ada.shape
    tn = n // 4
    return pl.pallas_call(
        _ada_kernel,
        grid=(depth, n // tn),
        in_specs=[pl.BlockSpec((MOD_ROWS, d), lambda l, j: (0, 0)),
                  pl.BlockSpec((None, d, tn), lambda l, j: (l, 0, j)),
                  pl.BlockSpec((None, 1, tn), lambda l, j: (l, 0, j))],
        out_specs=pl.BlockSpec((None, MOD_ROWS, tn), lambda l, j: (l, 0, j)),
        out_shape=jax.ShapeDtypeStruct((depth, MOD_ROWS, n), F32),
        compiler_params=_cparams(("arbitrary", "arbitrary")),
        name="ada_mod",
    )(cond, w_ada, b_ada.reshape(depth, 1, n))


def _norm_mod_kernel(x_ref, g_ref, mod_ref, hb_ref, hf_ref):
    h = _rms_mod(x_ref[...], g_ref[...], mod_ref[...], SH1, SC1)
    hb_ref[...] = h.astype(BF16)
    hf_ref[...] = h


def norm_mod(x_all, g, modtab):
    b, tt, d = x_all.shape
    nt = tt // ROW_TILE
    ctx_tile = nt - 1
    spec = pl.BlockSpec((None, ROW_TILE, d), lambda bi, i: (bi, i, 0))
    return pl.pallas_call(
        _norm_mod_kernel,
        grid=(b, nt),
        in_specs=[spec,
                  pl.BlockSpec((1, d), lambda bi, i: (0, 0)),
                  pl.BlockSpec((None, None, MOD_ROWS, d), lambda bi, i: (bi, jnp.where(i == ctx_tile, 0, 1), 0, 0))],
        out_specs=[spec, spec],
        out_shape=[jax.ShapeDtypeStruct((b, tt, d), BF16), jax.ShapeDtypeStruct((b, tt, d), F32)],
        compiler_params=_cparams(("parallel", "parallel")),
        name="norm_mod",
    )(x_all, g.reshape(1, d), modtab)


def _mm_kernel(a_ref, w_ref, o_ref, *, precision):
    o_ref[...] = jnp.dot(a_ref[...], w_ref[...], precision=precision,
                         preferred_element_type=F32).astype(o_ref.dtype)


def matmul(a, w, tm, tn, out_dtype=F32, precision=None):
    m, k = a.shape
    n = w.shape[1]
    return pl.pallas_call(
        functools.partial(_mm_kernel, precision=precision),
        grid=(n // tn, m // tm),
        in_specs=[pl.BlockSpec((tm, k), lambda j, i: (i, 0)),
                  pl.BlockSpec((k, tn), lambda j, i: (0, j))],
        out_specs=pl.BlockSpec((tm, tn), lambda j, i: (i, j)),
        out_shape=jax.ShapeDtypeStruct((m, n), out_dtype),
        compiler_params=_cparams(("parallel", "parallel")),
        name="matmul",
    )(a, w)


def _mlstm_kernel(qf, kf, vf, gif, gff, qb, kb, vb, gib, gfb, bi_ref, bf_ref, of, ob, ct_ref, m_ref):
    @pl.when(pl.program_id(1) == 0)
    def _():
        ct_ref[...] = jnp.zeros_like(ct_ref)
        m_ref[...] = jnp.zeros_like(m_ref)

    gates = (_mlstm_gates(gif, gff, bi_ref, bf_ref, m_ref.at[0], False),
             _mlstm_gates(gib, gfb, bi_ref, bf_ref, m_ref.at[1], True))
    refs = ((qf, kf, vf, of), (qb, kb, vb, ob))
    for h in range(MLSTM_HEADS):
        for d in range(2):
            _mlstm_head(h, *refs[d], ct_ref.at[d], gates[d], bool(d))


def _running_max(x, rev):
    n = x.shape[0]
    rowi = lax.broadcasted_iota(jnp.int32, (n, 1), 0)
    sh = 1
    while sh < n:
        if rev:
            x = jnp.maximum(x, jnp.where(rowi < n - sh, pltpu.roll(x, n - sh, 0), -jnp.inf))
        else:
            x = jnp.maximum(x, jnp.where(rowi >= sh, pltpu.roll(x, sh, 0), -jnp.inf))
        sh *= 2
    return x


def _mlstm_gates(gi_ref, gf_ref, bi_ref, bf_ref, m_ref, rev):
    L = ROW_TILE

    def cap(a):
        return GATE_SOFT_CAP * jnp.tanh(a / GATE_SOFT_CAP)

    ic = cap(gi_ref[...] + bi_ref[...])
    fp = cap(gf_ref[...] + bf_ref[...])
    lf = jnp.minimum(fp, 0.0) - jnp.log(1.0 + jnp.exp(-jnp.abs(fp)))
    row = lax.broadcasted_iota(jnp.int32, (L, L), 0)
    col = lax.broadcasted_iota(jnp.int32, (L, L), 1)
    tri = (col >= row) if rev else (col <= row)
    b = jnp.dot(jnp.where(tri, 1.0, 0.0), lf, precision=HIGHEST, preferred_element_type=F32)
    src = ic - b
    m_prev = m_ref[0:1, :]
    mu = jnp.maximum(m_prev, _running_max(src, rev))
    w_inter = jnp.exp(m_prev - mu)
    e_den = jnp.exp(-(b + mu))
    last = 0 if rev else L - 1
    b_last = b[last:last + 1, :]
    w_src = b_last - b + ic
    m_new = jnp.maximum(b_last + m_prev, jnp.max(w_src, axis=0, keepdims=True))
    a_src = jnp.exp(w_src - m_new)
    a_old = jnp.exp(b_last + m_prev - m_new)
    m_ref[...] = jnp.broadcast_to(m_new, m_ref.shape)
    return src.T, mu, w_inter, e_den, a_src, a_old, tri


def _mlstm_head(h, q_ref, k_ref, v_ref, o_ref, ct_ref, gates, rev):
    L = ROW_TILE
    H, DQK, DV = MLSTM_HEADS, MLSTM_DQK, MLSTM_DV
    src_t, mu_all, w_inter_all, e_den_all, a_src_all, a_old_all, tri = gates
    lane0 = lax.broadcasted_iota(jnp.int32, (L, LANES), 1) == 0
    ones_blk = jnp.where(lane0, 1.0, 0.0).astype(BF16)
    scale = DQK ** -0.5
    ln = (H if rev else 0) + h
    col = lambda a: a[:, ln:ln + 1]
    w_intra = jnp.exp(jnp.where(tri, src_t[ln:ln + 1, :] - col(mu_all), -jnp.inf))
    w_inter = col(w_inter_all)
    qb = (q_ref[:, h * DQK:(h + 1) * DQK] * scale).astype(BF16)
    kf = k_ref[:, h * DQK:(h + 1) * DQK]
    kb = kf.astype(BF16)
    vaug = jnp.concatenate([v_ref[:, h * DV:(h + 1) * DV].astype(BF16), ones_blk], axis=1)
    sqk = lax.dot_general(qb, kb, (((1,), (1,)), ((), ())), preferred_element_type=F32)
    sw = (sqk * w_intra).astype(BF16)
    ct = ct_ref[h]
    num_aug = (jnp.dot(sw, vaug, preferred_element_type=F32)
               + w_inter * jnp.dot(qb, ct.astype(BF16), preferred_element_type=F32))
    num = num_aug[:, :DV]
    den = num_aug[:, DV:DV + 1]
    o_ref[:, h * DV:(h + 1) * DV] = num / jnp.maximum(jnp.abs(den), col(e_den_all))
    ks = (kf * col(a_src_all)).astype(BF16)
    upd = lax.dot_general(ks, vaug, (((0,), (0,)), ((), ())), preferred_element_type=F32)
    ct_ref[h] = col(a_old_all) * ct + upd


def mlstm_scan(qkvo, gates, bias_i, bias_f):
    b, tt, _ = qkvo.shape
    nt = tt // ROW_TILE
    ctx_tile = nt - 1
    hq = MLSTM_HEADS * MLSTM_DQK
    hv = MLSTM_HEADS * MLSTM_DV
    order_f = lambda s: jnp.where(s == 0, ctx_tile, s - 1)
    order_b = lambda s: jnp.where(s == 0, ctx_tile, ctx_tile - s)

    def operands(order):
        return [pl.BlockSpec((None, ROW_TILE, hq), lambda bi, s: (bi, order(s), 0)),
                pl.BlockSpec((None, ROW_TILE, hq), lambda bi, s: (bi, order(s), 1)),
                pl.BlockSpec((None, ROW_TILE, hv), lambda bi, s: (bi, order(s), 1)),
                pl.BlockSpec((None, ROW_TILE, LANES), lambda bi, s: (bi, order(s), 0)),
                pl.BlockSpec((None, ROW_TILE, LANES), lambda bi, s: (bi, order(s), 1))]

    bias = pl.BlockSpec((1, LANES), lambda bi, s: (0, 0))
    return pl.pallas_call(
        _mlstm_kernel,
        grid=(b, nt),
        in_specs=operands(order_f) + operands(order_b) + [bias, bias],
        out_specs=[pl.BlockSpec((None, ROW_TILE, hv), lambda bi, s: (bi, order_f(s), 0)),
                   pl.BlockSpec((None, ROW_TILE, hv), lambda bi, s: (bi, order_b(s), 0))],
        out_shape=[jax.ShapeDtypeStruct((b, tt, hv), F32)] * 2,
        scratch_shapes=[pltpu.VMEM((2, MLSTM_HEADS, MLSTM_DQK, 2 * MLSTM_DV), F32),
                        pltpu.VMEM((2, 8, LANES), F32)],
        compiler_params=_cparams(("parallel", "arbitrary")),
        name="mlstm",
    )(*([qkvo] * 3 + [gates] * 2) * 2, bias_i, bias_f)


OP_KK, OP_R, OP_WF, OP_BF, OP_KF, OP_WB, OP_BB, OP_KB = range(8)
N_OPS = 8


def _rwkv_prep_kernel(p_ref, hp_ref, hn_ref, mu_ref, w0_ref, w2_ref, a0_ref, a2_ref, g2_ref, kk_ref, ka_ref,
                      rk_ref, ops_o, v_o, g_o, bonus_o, xs_ref):
    T = ROW_TILE
    i = pl.program_id(1)
    n_lat = pl.num_programs(1) - 1
    x = p_ref[...]
    W = x.shape[1]
    rowi = lax.broadcasted_iota(jnp.int32, (T, 1), 0)
    coli = lax.broadcasted_iota(jnp.int32, (1, W), 1)
    prev = pltpu.roll(x, 1, 0)
    nxt = pltpu.roll(x, T - 1, 0)

    @pl.when(i < n_lat)
    def _():
        gc = rowi & (GRID_W - 1)
        left = jnp.where(gc == 0, 0.0, prev)
        right = jnp.where(gc == GRID_W - 1, 0.0, nxt)
        hp = jnp.where(i == 0, 0.0, hp_ref[...])
        hn = jnp.where(i == n_lat - 1, 0.0, hn_ref[...])
        up = jnp.concatenate([hp, x[:T - GRID_W]], axis=0)
        down = jnp.concatenate([x[GRID_W:], hn], axis=0)
        q = W // 4
        xs_ref[...] = jnp.where(coli < q, left, jnp.where(coli < 2 * q, right, jnp.where(coli < 3 * q, up, down)))

    @pl.when(i == n_lat)
    def _():
        xs_ref[...] = jnp.where(coli < W // 2, jnp.where(rowi == 0, 0.0, prev), jnp.where(rowi == T - 1, 0.0, nxt))

    xm = x + (xs_ref[...] - x) * mu_ref[...]
    C = RWKV_HEADS * RWKV_N
    r = xm[:, 0:C]
    k = xm[:, C:2 * C]
    v = xm[:, 2 * C:3 * C]
    wd = jnp.tanh(xm[:, 3 * C:3 * C + 2 * DECAY_LORA]).astype(BF16)
    ad = xm[:, 3 * C + 2 * DECAY_LORA:3 * C + 2 * DECAY_LORA + 2 * AAA_LORA].astype(BF16)
    gd = _sigmoid(xm[:, 3 * C + 2 * DECAY_LORA + 2 * AAA_LORA:]).astype(BF16)
    bd = _block_diag_ones(RWKV_N)

    kk = k * kk_ref[...]
    nrm = jnp.sqrt(_group_sum(kk * kk, bd))
    kk = kk / jnp.maximum(nrm, 1e-12)
    u = w0_ref[...] + jnp.dot(wd, w2_ref[...], preferred_element_type=F32)
    decay = jnp.exp(-np.float32(np.exp(-0.5)) * _sigmoid(u))
    a = _sigmoid(a0_ref[...] + jnp.dot(ad, a2_ref[...], preferred_element_type=F32))
    ka = ka_ref[...]
    kd_f = k * (1.0 + (a[:, :C] - 1.0) * ka)
    kd_b = k * (1.0 + (a[:, C:] - 1.0) * ka)
    ops_o[OP_R] = r
    v_o[...] = v
    ops_o[OP_KK] = kk
    ops_o[OP_WF] = decay[:, :C]
    ops_o[OP_WB] = decay[:, C:]
    ops_o[OP_BF] = kk * a[:, :C]
    ops_o[OP_BB] = kk * a[:, C:]
    ops_o[OP_KF] = kd_f
    ops_o[OP_KB] = kd_b
    g_o[...] = jnp.dot(gd, g2_ref[...], preferred_element_type=F32)
    bonus_o[...] = _group_sum(r * (kd_f + kd_b) * rk_ref[...], bd) * v


def rwkv_prepare(p, mu, w0cat, w2cat, a0cat, a2cat, g2, k_k, k_a, r_k):
    b, tt, w = p.shape
    nt = tt // ROW_TILE
    n_lat = nt - 1
    hpt = ROW_TILE // GRID_W
    c = RWKV_HEADS * RWKV_N
    n_halo = tt // GRID_W
    const = lambda shape: pl.BlockSpec(shape, lambda bi, i: (0,) * len(shape))
    out_spec = pl.BlockSpec((None, ROW_TILE, c), lambda bi, i: (bi, i, 0))
    return pl.pallas_call(
        _rwkv_prep_kernel,
        grid=(b, nt),
        in_specs=[pl.BlockSpec((None, ROW_TILE, w), lambda bi, i: (bi, i, 0)),
                  pl.BlockSpec((None, GRID_W, w), lambda bi, i: (bi, jnp.maximum(i * hpt - 1, 0), 0)),
                  pl.BlockSpec((None, GRID_W, w), lambda bi, i: (bi, jnp.minimum(i * hpt + hpt, n_halo - 1), 0)),
                  const((1, w)), const((1, 2 * c)), const((2 * DECAY_LORA, 2 * c)), const((1, 2 * c)),
                  const((2 * AAA_LORA, 2 * c)), const((GATE_LORA, c)), const((1, c)), const((1, c)), const((1, c))],
        out_specs=[pl.BlockSpec((N_OPS, None, ROW_TILE, c), lambda bi, i: (0, bi, i, 0))] + [out_spec] * 3,
        out_shape=[jax.ShapeDtypeStruct((N_OPS, b, tt, c), F32)] + [jax.ShapeDtypeStruct((b, tt, c), F32)] * 3,
        scratch_shapes=[pltpu.VMEM((ROW_TILE, w), F32)],
        compiler_params=_cparams(("parallel", "parallel")),
        name="rwkv_prepare",
    )(p, p, p, mu, w0cat, w2cat, a0cat, a2cat, g2, k_k, k_a, r_k)


RELAYOUT_ROWS = 128
V_HALF = RWKV_N // 2


def _rows_to_heads(x_ref, zs_ref):
    for b in range(x_ref.shape[0]):
        for cb in range(x_ref.shape[2] // LANES):
            r0 = (b * (x_ref.shape[2] // LANES) + cb) * LANES
            zs_ref[r0:r0 + LANES, :] = x_ref[b, :, cb * LANES:(cb + 1) * LANES].T


def _to_scan_k_kernel(x_ref, o_ref, zs_ref):
    _rows_to_heads(x_ref, zs_ref)
    nbh = zs_ref.shape[0] // RWKV_N
    for k in range(RWKV_N):
        g = zs_ref[pl.ds(k, nbh, stride=RWKV_N), :]
        o_ref[k] = jnp.concatenate([g, g], axis=0).T


def to_scan_k(ops):
    n_ops, b, tt, c = ops.shape
    return pl.pallas_call(
        _to_scan_k_kernel,
        grid=(n_ops, tt // RELAYOUT_ROWS),
        in_specs=[pl.BlockSpec((None, b, RELAYOUT_ROWS, c), lambda o, i: (o, 0, i, 0))],
        out_specs=pl.BlockSpec((None, RWKV_N, RELAYOUT_ROWS, LANES), lambda o, i: (o, 0, i, 0)),
        out_shape=jax.ShapeDtypeStruct((n_ops, RWKV_N, tt, LANES), F32),
        scratch_shapes=[pltpu.VMEM((b * c, RELAYOUT_ROWS), F32)],
        compiler_params=_cparams(("parallel", "parallel")),
        name="to_scan_k",
    )(ops)


def _to_scan_v_kernel(x_ref, o_ref, zs_ref):
    _rows_to_heads(x_ref, zs_ref)
    nbh = zs_ref.shape[0] // RWKV_N
    for v in range(V_HALF):
        m = jnp.concatenate([zs_ref[pl.ds(v, nbh, stride=RWKV_N), :],
                             zs_ref[pl.ds(V_HALF + v, nbh, stride=RWKV_N), :]], axis=0)
        o_ref[:, v, :] = m.T


def to_scan_v(x):
    b, tt, c = x.shape
    return pl.pallas_call(
        _to_scan_v_kernel,
        grid=(tt // RELAYOUT_ROWS,),
        in_specs=[pl.BlockSpec((b, RELAYOUT_ROWS, c), lambda i: (0, i, 0))],
        out_specs=pl.BlockSpec((RELAYOUT_ROWS, V_HALF, LANES), lambda i: (i, 0, 0)),
        out_shape=jax.ShapeDtypeStruct((tt, V_HALF, LANES), F32),
        scratch_shapes=[pltpu.VMEM((b * c, RELAYOUT_ROWS), F32)],
        compiler_params=_cparams(("parallel",)),
        name="to_scan_v",
    )(x)


def _from_scan_kernel(yf_ref, yb_ref, o_ref, zs_ref):
    nbh = zs_ref.shape[0] // RWKV_N
    for v in range(V_HALF):
        t = (yf_ref[:, v, :] + yb_ref[:, v, :]).T
        zs_ref[pl.ds(v, nbh, stride=RWKV_N), :] = t[:nbh]
        zs_ref[pl.ds(V_HALF + v, nbh, stride=RWKV_N), :] = t[nbh:]
    for b in range(o_ref.shape[0]):
        for cb in range(o_ref.shape[2] // LANES):
            r0 = (b * (o_ref.shape[2] // LANES) + cb) * LANES
            o_ref[b, :, cb * LANES:(cb + 1) * LANES] = zs_ref[r0:r0 + LANES, :].T


def from_scan(yf, yb, b):
    tt = yf.shape[0]
    c = RWKV_HEADS * RWKV_N
    spec = pl.BlockSpec((RELAYOUT_ROWS, V_HALF, LANES), lambda i: (i, 0, 0))
    return pl.pallas_call(
        _from_scan_kernel,
        grid=(tt // RELAYOUT_ROWS,),
        in_specs=[spec, spec],
        out_specs=pl.BlockSpec((b, RELAYOUT_ROWS, c), lambda i: (0, i, 0)),
        out_shape=jax.ShapeDtypeStruct((b, tt, c), F32),
        scratch_shapes=[pltpu.VMEM((b * c, RELAYOUT_ROWS), F32)],
        compiler_params=_cparams(("parallel",)),
        name="from_scan",
    )(yf, yb)


SCAN_ROWS = 32


def _rwkv_scan_kernel(kk_f, kkn_f, r_f, w_f, b_f, k_f, kk_b, kkn_b, r_b, w_b, b_b, k_b, v_f, v_b, y_f, y_b,
                      s_ref, sa_ref, kkx_f, kkx_b):
    N, TB, PAD = RWKV_N, SCAN_ROWS, 8

    @pl.when(pl.program_id(0) == 0)
    def _():
        s_ref[...] = jnp.zeros_like(s_ref)
        sa_ref[...] = jnp.zeros_like(sa_ref)

    kkx_f[:, 0:TB, :] = kk_f[...]
    kkx_f[:, TB:TB + PAD, :] = kkn_f[:, 0:PAD, :]
    kkx_b[:, PAD:PAD + TB, :] = kk_b[...]
    kkx_b[:, 0:PAD, :] = kkn_b[:, TB - PAD:TB, :]

    dirs = ((kkx_f, r_f, w_f, b_f, k_f, v_f, y_f), (kkx_b, r_b, w_b, b_b, k_b, v_b, y_b))

    def step(j, carry):
        for d, (kkx_r, r_r, w_r, b_r, k_r, v_r, y_r) in enumerate(dirs):
            row, nxt = (j, j + 1) if d == 0 else (TB - 1 - j, PAD + TB - 2 - j)
            op = lambda ref, kx, rw: ref[kx, pl.ds(rw, 1), :]
            sa = sa_ref[d]
            vt = v_r[row]
            acc_sa = [jnp.zeros((V_HALF, LANES), F32), jnp.zeros((V_HALF, LANES), F32)]
            acc_y = [jnp.zeros((V_HALF, LANES), F32), jnp.zeros((V_HALF, LANES), F32)]
            for kx in range(N):
                sk = s_ref[d, kx] * op(w_r, kx, row) - sa * op(b_r, kx, row) + vt * op(k_r, kx, row)
                s_ref[d, kx] = sk
                acc_sa[kx % 2] = acc_sa[kx % 2] + sk * op(kkx_r, kx, nxt)
                acc_y[kx % 2] = acc_y[kx % 2] + sk * op(r_r, kx, row)
            sa_ref[d] = acc_sa[0] + acc_sa[1]
            y_r[row] = acc_y[0] + acc_y[1]
        return carry

    lax.fori_loop(0, SCAN_ROWS, step, 0)


def rwkv_scan(ops_s, v_s, seq):
    _, n, tt, lanes = ops_s.shape
    n_lat = seq // SCAN_ROWS
    n_blk = tt // SCAN_ROWS
    n_ctx = n_blk - n_lat
    blk_f = lambda s: jnp.where(s < n_ctx, n_lat + s, s - n_ctx)
    blk_b = lambda s: n_blk - 1 - s
    kspec = lambda o, blk: pl.BlockSpec((None, n, SCAN_ROWS, lanes), lambda s: (o, 0, blk(s), 0))
    vspec = lambda blk: pl.BlockSpec((SCAN_ROWS, V_HALF, lanes), lambda s: (blk(s), 0, 0))
    nxt = lambda blk: (lambda s: blk(jnp.minimum(s + 1, n_blk - 1)))
    fwd = [kspec(OP_KK, blk_f), kspec(OP_KK, nxt(blk_f))] + [kspec(o, blk_f) for o in (OP_R, OP_WF, OP_BF, OP_KF)]
    bwd = [kspec(OP_KK, blk_b), kspec(OP_KK, nxt(blk_b))] + [kspec(o, blk_b) for o in (OP_R, OP_WB, OP_BB, OP_KB)]
    return pl.pallas_call(
        _rwkv_scan_kernel,
        grid=(n_blk,),
        in_specs=fwd + bwd + [vspec(blk_f), vspec(blk_b)],
        out_specs=[vspec(blk_f), vspec(blk_b)],
        out_shape=[jax.ShapeDtypeStruct((tt, V_HALF, lanes), F32)] * 2,
        scratch_shapes=[pltpu.VMEM((2, n, V_HALF, lanes), F32), pltpu.VMEM((2, V_HALF, lanes), F32),
                        pltpu.VMEM((n, SCAN_ROWS + 8, lanes), F32), pltpu.VMEM((n, SCAN_ROWS + 8, lanes), F32)],
        compiler_params=_cparams(("arbitrary",)),
        name="rwkv_scan",
    )(*([ops_s] * 12), v_s, v_s)


def _merge_kernel(hf_ref, hb_ref, o_ref, y_ref, bonus_ref, g_ref, mg_ref, x_ref, mod_ref, gm_ref, lnw_ref,
                  lnb_ref, wpm_ref, wpr_ref, wout_ref, out_ref):
    c = RWKV_HEADS * RWKV_N
    hm = hf_ref[...] + hb_ref[...]
    ms = _group_sum(hm * hm, _block_diag_ones(MLSTM_DV)) * (1.0 / MLSTM_DV)
    hmn = hm * lax.rsqrt(ms + NORM_EPS) * gm_ref[...] * _sigmoid(o_ref[...])
    ym = jnp.dot(hmn.astype(BF16), wpm_ref[...], preferred_element_type=F32)
    bd = _block_diag_ones(RWKV_N)
    y = y_ref[...]
    yc = y - _group_sum(y, bd) * (1.0 / RWKV_N)
    var = _group_sum(yc * yc, bd) * (1.0 / RWKV_N)
    yn = yc * lax.rsqrt(var + GN_EPS) * lnw_ref[...] + lnb_ref[...]
    yr_in = ((yn + bonus_ref[...]) * g_ref[...]).astype(BF16)
    yr = jnp.dot(yr_in, wpr_ref[...], preferred_element_type=F32)
    mg = mg_ref[...]
    z = _sigmoid(mg[:, :c]) * ym + _sigmoid(mg[:, c:]) * yr
    yy = jnp.dot(z.astype(BF16), wout_ref[...], preferred_element_type=F32)
    out_ref[...] = x_ref[...] + mod_ref[GT1:GT1 + 1, :] * yy


def merge(hf, hb, qkvo, y, bonus, g, mg, x_all, modtab, gm, lnw, lnb, wpm, wpr, wout, nt_out):
    b, tt, d = x_all.shape
    ctx_tile = tt // ROW_TILE - 1
    row = lambda width, colblk=0: pl.BlockSpec((None, ROW_TILE, width), lambda bi, i: (bi, i, colblk))
    const = lambda shape: pl.BlockSpec(shape, lambda bi, i: (0,) * len(shape))
    return pl.pallas_call(
        _merge_kernel,
        grid=(b, nt_out),
        in_specs=[row(d), row(d), row(d, 2), row(d), row(d), row(d), row(2 * d), row(d),
                  pl.BlockSpec((None, None, MOD_ROWS, d), lambda bi, i: (bi, jnp.where(i == ctx_tile, 0, 1), 0, 0)),
                  const((1, d)), const((1, d)), const((1, d)), const((d, d)), const((d, d)), const((d, d))],
        out_specs=row(d),
        out_shape=jax.ShapeDtypeStruct((b, nt_out * ROW_TILE, d), F32),
        compiler_params=_cparams(("parallel", "parallel")),
        name="merge",
    )(hf, hb, qkvo, y, bonus, g, mg, x_all, modtab, gm, lnw, lnb, wpm, wpr, wout)


def _ffn_kernel(x_ref, g_ref, mod_ref, wg_ref, wu_ref, wd_ref, o_ref, h_scr, acc_scr):
    j = pl.program_id(2)

    @pl.when(j == 0)
    def _():
        h_scr[...] = _rms_mod(x_ref[...], g_ref[...], mod_ref[...], SH2, SC2).astype(BF16)
        acc_scr[...] = jnp.zeros_like(acc_scr)

    h = h_scr[...]
    a = jnp.dot(h, wg_ref[...], preferred_element_type=F32)
    u = jnp.dot(h, wu_ref[...], preferred_element_type=F32)
    t = (a * _sigmoid(a) * u).astype(BF16)
    acc_scr[...] += jnp.dot(t, wd_ref[...], preferred_element_type=F32)

    @pl.when(j == pl.num_programs(2) - 1)
    def _():
        o_ref[...] = x_ref[...] + mod_ref[GT2:GT2 + 1, :] * acc_scr[...]


def ffn_dense(x_all, g, modtab, wg, wu, wd, tf):
    b, tt, d = x_all.shape
    nt = tt // ROW_TILE
    ctx_tile = nt - 1
    f = wg.shape[1]
    row = pl.BlockSpec((None, ROW_TILE, d), lambda bi, i, j: (bi, i, 0))
    return pl.pallas_call(
        _ffn_kernel,
        grid=(b, nt, f // tf),
        in_specs=[row,
                  pl.BlockSpec((1, d), lambda bi, i, j: (0, 0)),
                  pl.BlockSpec((None, None, MOD_ROWS, d),
                               lambda bi, i, j: (bi, jnp.where(i == ctx_tile, 0, 1), 0, 0)),
                  pl.BlockSpec((d, tf), lambda bi, i, j: (0, j)),
                  pl.BlockSpec((d, tf), lambda bi, i, j: (0, j)),
                  pl.BlockSpec((tf, d), lambda bi, i, j: (j, 0))],
        out_specs=row,
        out_shape=jax.ShapeDtypeStruct((b, tt, d), F32),
        scratch_shapes=[pltpu.VMEM((ROW_TILE, d), BF16), pltpu.VMEM((ROW_TILE, d), F32)],
        compiler_params=_cparams(("parallel", "parallel", "arbitrary")),
        name="ffn_dense",
    )(x_all, g.reshape(1, d), modtab, wg, wu, wd)


MOE_ROWS = 512


TOP_K = 2
MOE_TOKENS = 256
SEL_E1, SEL_E2, SEL_P1, SEL_P2 = range(4)


def _moe_route_kernel(x_ref, g_ref, mod_ref, wr_ref, h_o, sel_o):
    h = _rms_mod(x_ref[...], g_ref[...], mod_ref[...], SH2, SC2)
    h_o[...] = h
    logits = jnp.dot(h, wr_ref[...], precision=HIGHEST, preferred_element_type=F32)
    lane = lax.broadcasted_iota(jnp.int32, logits.shape, 1)
    logits = jnp.where(lane < N_EXPERTS, logits, -jnp.inf)
    m1 = jnp.max(logits, axis=-1, keepdims=True)
    i1 = jnp.min(jnp.where(logits == m1, lane, LANES), axis=-1, keepdims=True)
    rest = jnp.where(lane == i1, -jnp.inf, logits)
    m2 = jnp.max(rest, axis=-1, keepdims=True)
    i2 = jnp.min(jnp.where(rest == m2, lane, LANES), axis=-1, keepdims=True)
    e21 = jnp.exp(m2 - m1)
    p1 = 1.0 / (1.0 + e21)
    sel_o[...] = jnp.where(lane == SEL_E1, i1.astype(F32),
                           jnp.where(lane == SEL_E2, i2.astype(F32),
                                     jnp.where(lane == SEL_P1, p1, jnp.where(lane == SEL_P2, e21 * p1, 0.0))))


def moe_route(x_lat, g, modtab, w_router):
    b, s, d = x_lat.shape
    nt = s // MOE_ROWS
    return pl.pallas_call(
        _moe_route_kernel,
        grid=(b, nt),
        in_specs=[pl.BlockSpec((None, MOE_ROWS, d), lambda bi, i: (bi, i, 0)),
                  pl.BlockSpec((1, d), lambda bi, i: (0, 0)),
                  pl.BlockSpec((None, None, MOD_ROWS, d), lambda bi, i: (bi, 1, 0, 0)),
                  pl.BlockSpec((d, LANES), lambda bi, i: (0, 0))],
        out_specs=[pl.BlockSpec((MOE_ROWS, d), lambda bi, i: (bi * nt + i, 0)),
                   pl.BlockSpec((MOE_ROWS, LANES), lambda bi, i: (bi * nt + i, 0))],
        out_shape=[jax.ShapeDtypeStruct((b * s, d), F32), jax.ShapeDtypeStruct((b * s, LANES), F32)],
        compiler_params=_cparams(("parallel", "parallel")),
        name="moe_route",
    )(x_lat, g.reshape(1, d), modtab, w_router)


def _row_copy(src, src_row, dst, dst_row, sem):
    return pltpu.make_async_copy(src.at[pl.ds(src_row, 1)], dst.at[pl.ds(dst_row, 1)], sem)


def _moe_dispatch_kernel(src_ref, h_hbm, xg_ref, sem):
    n = src_ref.shape[-1]

    def start(r, c):
        _row_copy(h_hbm, src_ref[0, r], xg_ref, r, sem).start()
        return c

    def wait(r, c):
        _row_copy(h_hbm, 0, xg_ref, 0, sem).wait()
        return c

    lax.fori_loop(0, n, start, 0, unroll=8)
    lax.fori_loop(0, n, wait, 0, unroll=8)


def moe_dispatch(src, h):
    rows = src.shape[0]
    d = h.shape[1]
    return pl.pallas_call(
        _moe_dispatch_kernel,
        grid=(rows // MOE_TOKENS,),
        in_specs=[pl.BlockSpec((None, 1, MOE_TOKENS), lambda i: (i, 0, 0), memory_space=pltpu.SMEM),
                  pl.BlockSpec(memory_space=pl.ANY)],
        out_specs=pl.BlockSpec((MOE_TOKENS, d), lambda i: (i, 0)),
        out_shape=jax.ShapeDtypeStruct((rows, d), F32),
        scratch_shapes=[pltpu.SemaphoreType.DMA(())],
        compiler_params=_cparams(("arbitrary",)),
        name="moe_dispatch",
    )(src.reshape(rows // MOE_TOKENS, 1, MOE_TOKENS), h)


def _moe_experts_kernel(te_ref, nu_ref, xg_ref, wg_ref, wu_ref, wd_ref, y_ref, h_scr, acc_scr):
    del te_ref
    i = pl.program_id(0)
    j = pl.program_id(1)
    used = i < nu_ref[0]
    last = j == pl.num_programs(1) - 1

    @pl.when(jnp.logical_and(used, j == 0))
    def _():
        h_scr[...] = xg_ref[...].astype(BF16)
        acc_scr[...] = jnp.zeros_like(acc_scr)

    @pl.when(used)
    def _():
        h = h_scr[...]
        a = jnp.dot(h, wg_ref[...], preferred_element_type=F32)
        u = jnp.dot(h, wu_ref[...], preferred_element_type=F32)
        t = (a * _sigmoid(a) * u).astype(BF16)
        acc_scr[...] += jnp.dot(t, wd_ref[...], preferred_element_type=F32)

    @pl.when(jnp.logical_and(used, last))
    def _():
        y_ref[...] = acc_scr[...]

    @pl.when(jnp.logical_and(jnp.logical_not(used), last))
    def _():
        y_ref[...] = jnp.zeros_like(y_ref)


def moe_experts(tile_expert, n_used, xg, wg, wu, wd, tf):
    rows, d = xg.shape
    f = wg.shape[2]
    return pl.pallas_call(
        _moe_experts_kernel,
        grid_spec=pltpu.PrefetchScalarGridSpec(
            num_scalar_prefetch=2,
            grid=(rows // MOE_ROWS, f // tf),
            in_specs=[pl.BlockSpec((MOE_ROWS, d), lambda i, j, te, nu: (i, 0)),
                      pl.BlockSpec((None, d, tf), lambda i, j, te, nu: (te[i], 0, j)),
                      pl.BlockSpec((None, d, tf), lambda i, j, te, nu: (te[i], 0, j)),
                      pl.BlockSpec((None, tf, d), lambda i, j, te, nu: (te[i], j, 0))],
            out_specs=pl.BlockSpec((MOE_ROWS, d), lambda i, j, te, nu: (i, 0)),
            scratch_shapes=[pltpu.VMEM((MOE_ROWS, d), BF16), pltpu.VMEM((MOE_ROWS, d), F32)]),
        out_shape=jax.ShapeDtypeStruct((rows, d), F32),
        compiler_params=_cparams(("arbitrary", "arbitrary")),
        name="moe_experts",
    )(tile_expert, n_used, xg, wg, wu, wd)


def _moe_combine_kernel(pos_ref, sel_ref, x_ref, mod_ref, gfin_ref, y_hbm, o_ref, ybuf, sem):
    n = pos_ref.shape[-1]

    def start(a, c):
        _row_copy(y_hbm, pos_ref[0, a], ybuf.at[a & 1], lax.shift_right_logical(a, 1), sem).start()
        return c

    def wait(a, c):
        _row_copy(y_hbm, 0, ybuf.at[0], 0, sem).wait()
        return c

    lax.fori_loop(0, n, start, 0, unroll=8)
    lax.fori_loop(0, n, wait, 0, unroll=8)
    sel = sel_ref[...]
    yy = sel[:, SEL_P1:SEL_P1 + 1] * ybuf[0] + sel[:, SEL_P2:SEL_P2 + 1] * ybuf[1]
    xo = x_ref[...] + mod_ref[GT2:GT2 + 1, :] * yy
    ms = jnp.mean(xo * xo, axis=-1, keepdims=True)
    o_ref[...] = xo * lax.rsqrt(ms + NORM_EPS) * gfin_ref[...]


def moe_combine(pos, sel, x_lat, modtab, g_final, y):
    b, s, d = x_lat.shape
    nt = s // MOE_TOKENS
    n = TOP_K * MOE_TOKENS
    row = pl.BlockSpec((None, MOE_TOKENS, d), lambda bi, i: (bi, i, 0))
    return pl.pallas_call(
        _moe_combine_kernel,
        grid=(b, nt),
        in_specs=[pl.BlockSpec((None, 1, n), lambda bi, i: (bi * nt + i, 0, 0), memory_space=pltpu.SMEM),
                  pl.BlockSpec((MOE_TOKENS, LANES), lambda bi, i: (bi * nt + i, 0)),
                  row,
                  pl.BlockSpec((None, None, MOD_ROWS, d), lambda bi, i: (bi, 1, 0, 0)),
                  pl.BlockSpec((1, d), lambda bi, i: (0, 0)),
                  pl.BlockSpec(memory_space=pl.ANY)],
        out_specs=row,
        out_shape=jax.ShapeDtypeStruct((b, s, d), F32),
        scratch_shapes=[pltpu.VMEM((TOP_K, MOE_TOKENS, d), F32), pltpu.SemaphoreType.DMA(())],
        compiler_params=_cparams(("arbitrary", "arbitrary")),
        name="moe_combine",
    )(pos.reshape(b * nt, 1, n), sel, x_lat, modtab, g_final.reshape(1, d), y)


def moe_final(x_lat, g, modtab, w_router, wg, wu, wd, g_final, tf):
    b, s, d = x_lat.shape
    m = b * s
    ne = wg.shape[0]
    h, sel = moe_route(x_lat, g, modtab, w_router)
    e = sel[:, SEL_E1:SEL_E2 + 1].astype(jnp.int32).reshape(m * TOP_K)
    onehot = (e[:, None] == jnp.arange(ne, dtype=jnp.int32)[None, :]).astype(jnp.int32)
    running = jnp.cumsum(onehot, axis=0)
    counts = running[-1]
    padded = (counts + MOE_ROWS - 1) // MOE_ROWS * MOE_ROWS
    group_end = jnp.cumsum(padded)
    group_start = group_end - padded
    pos = jnp.sum(onehot * (group_start[None, :] + running - 1), axis=1).astype(jnp.int32)
    n_tiles = m * TOP_K // MOE_ROWS + ne
    tile_expert = jnp.minimum(
        jnp.sum(jnp.arange(n_tiles, dtype=jnp.int32)[:, None] >= (group_end // MOE_ROWS)[None, :], axis=1),
        ne - 1).astype(jnp.int32)
    n_used = (group_end[-1:] // MOE_ROWS).astype(jnp.int32)
    order = jnp.argsort(e, stable=True).astype(jnp.int32)
    rows = jnp.arange(n_tiles * MOE_ROWS, dtype=jnp.int32)
    row_expert = tile_expert[rows // MOE_ROWS]
    offset = rows - group_start[row_expert]
    choice = order[jnp.clip((jnp.cumsum(counts) - counts)[row_expert] + offset, 0, m * TOP_K - 1)]
    src = jnp.where(offset < counts[row_expert], choice // TOP_K, 0).astype(jnp.int32)
    xg = moe_dispatch(src, h)
    y = moe_experts(tile_expert, n_used, xg, wg, wu, wd, tf)
    return moe_combine(pos, sel, x_lat, modtab, g_final, y)


def kernel(x, c, ctx, c_ctx, w_ada, b_ada, g_norm_mix, g_norm_ffn, w_in, b_mlstm_gate, g_mlstm_norm, mu_shift, w0, w2, a0, a2, g2, k_k, k_a, r_k, ln_w, ln_b, w_proj_mlstm, w_proj_rwkv, w_out, w_ff_gate, w_ff_up, w_ff_down, w_router, w_exp_gate, w_exp_up, w_exp_down, g_final):
    b, seq, d = x.shape
    ctx_len = ctx.shape[1]
    depth = w_ada.shape[0]
    assert ctx_len == ROW_TILE and seq % ROW_TILE == 0 and seq % MOE_ROWS == 0 and 2 * b * RWKV_HEADS == LANES
    tt = seq + ctx_len
    m_all = b * tt
    H = MLSTM_HEADS
    cr = RWKV_HEADS * RWKV_N
    hq, hv = H * MLSTM_DQK, H * MLSTM_DV

    x_all = jnp.concatenate([x, ctx], axis=1)
    cond = jnp.concatenate([c, c_ctx[None, :], jnp.zeros((MOD_ROWS - b - 1, d), F32)], axis=0)
    mod_all = ada_mod(cond, w_ada, b_ada).reshape(depth, MOD_ROWS, 6, d)

    o_gate = 2 * hq + 2 * hv
    o_rwkv = o_gate + 4 * H
    rwkv_w = 3 * cr + 2 * DECAY_LORA + 2 * AAA_LORA + GATE_LORA
    o_mg = o_rwkv + rwkv_w

    out = None
    for l in range(depth):
        need_ctx = l < depth - 1
        pad = jnp.zeros((b, MOD_ROWS - 6, d), F32)
        mod_lat = jnp.concatenate([mod_all[l, :b], pad], axis=1)
        mod_ctx = jnp.broadcast_to(jnp.concatenate([mod_all[l, b], pad[0]], axis=0), (b, MOD_ROWS, d))
        modtab = jnp.stack([mod_ctx, mod_lat], axis=1)

        wl = w_in[l]
        w_qkvo = wl[:, :o_gate].astype(BF16)
        wgt = wl[:, o_gate:o_rwkv].reshape(d, 4, H)
        zpad = jnp.zeros((d, LANES - 2 * H), F32)
        w_gates = jnp.concatenate([wgt[:, 0], wgt[:, 2], zpad, wgt[:, 1], wgt[:, 3], zpad], axis=1)
        bg = b_mlstm_gate[l]
        lpad = jnp.zeros((LANES - 2 * H,), F32)
        bias_i = jnp.concatenate([bg[0], bg[2], lpad]).reshape(1, LANES)
        bias_f = jnp.concatenate([bg[1], bg[3], lpad]).reshape(1, LANES)
        w_rwkv = wl[:, o_rwkv:o_mg].astype(BF16)
        w_mg = wl[:, o_mg:].astype(BF16)

        hb16, hf32 = norm_mod(x_all, g_norm_mix[l], modtab)
        hb16 = hb16.reshape(m_all, d)
        qkvo = matmul(hb16, w_qkvo, 512, 1024).reshape(b, tt, o_gate)
        gates = matmul(hf32.reshape(m_all, d), w_gates, 512, 2 * LANES, precision=HIGHEST).reshape(b, tt, 2 * LANES)
        prw = matmul(hb16, w_rwkv, 512, rwkv_w // 3).reshape(b, tt, rwkv_w)
        mg = matmul(hb16, w_mg, 512, 1024).reshape(b, tt, 2 * d)

        h_f, h_b = mlstm_scan(qkvo, gates, bias_i, bias_f)

        zl = jnp.zeros((DECAY_LORA, cr), F32)
        w2cat = jnp.concatenate([jnp.concatenate([w2[l, 0], zl], axis=1),
                                 jnp.concatenate([zl, w2[l, 1]], axis=1)], axis=0).astype(BF16)
        a2cat = jnp.concatenate([jnp.concatenate([a2[l, 0], zl], axis=1),
                                 jnp.concatenate([zl, a2[l, 1]], axis=1)], axis=0).astype(BF16)
        ops_, v_, g_, bonus_ = rwkv_prepare(
            prw, mu_shift[l].reshape(1, rwkv_w), w0[l].reshape(1, 2 * cr), w2cat, a0[l].reshape(1, 2 * cr), a2cat,
            g2[l].astype(BF16), k_k[l].reshape(1, cr), k_a[l].reshape(1, cr), r_k[l].reshape(1, cr))
        y_f, y_b = rwkv_scan(to_scan_k(ops_), to_scan_v(v_), seq)
        y_ = from_scan(y_f, y_b, b)

        nt_out = tt // ROW_TILE if need_ctx else seq // ROW_TILE
        x_mid = merge(h_f, h_b, qkvo, y_, bonus_, g_, mg, x_all, modtab, g_mlstm_norm[l].reshape(1, hv),
                      ln_w[l].reshape(1, cr), ln_b[l].reshape(1, cr), w_proj_mlstm[l].astype(BF16),
                      w_proj_rwkv[l].astype(BF16), w_out[l].astype(BF16), nt_out)

        i = l // 2
        if l % 2 == 0:
            assert need_ctx
            x_all = ffn_dense(x_mid, g_norm_ffn[l], modtab, w_ff_gate[i].astype(BF16), w_ff_up[i].astype(BF16),
                              w_ff_down[i].astype(BF16), w_ff_gate.shape[2] // 2)
        else:
            assert not need_ctx
            wr = jnp.concatenate([w_router[i], jnp.zeros((d, LANES - N_EXPERTS), F32)], axis=1)
            out = moe_final(x_mid, g_norm_ffn[l], modtab, wr, w_exp_gate[i].astype(BF16), w_exp_up[i].astype(BF16),
                            w_exp_down[i].astype(BF16), g_final, w_exp_gate.shape[3] // 2)
    return out
```

```python
import functools

import jax
import jax.numpy as jnp
import numpy as np
from jax import lax
from jax.experimental import pallas as pl
from jax.experimental.pallas import tpu as pltpu

F32 = jnp.float32
BF16 = jnp.bfloat16
HIGHEST = lax.Precision.HIGHEST

GRID_W = 64
MLSTM_HEADS = 8
MLSTM_DQK = 64
MLSTM_DV = 128
GATE_SOFT_CAP = 15.0
RWKV_HEADS = 16
RWKV_N = 64
DECAY_LORA = 64
AAA_LORA = 64
GATE_LORA = 128
GN_EPS = 64e-5
N_EXPERTS = 8
NORM_EPS = 1e-6

LANES = 128
MXU_DIM = 256
ROW_TILE = 256
VMEM_LIMIT = 56 * 1024 * 1024

SH1, SC1, GT1, SH2, SC2, GT2 = range(6)
MOD_ROWS = 8


def _cparams(sem):
    return pltpu.CompilerParams(dimension_semantics=sem, vmem_limit_bytes=VMEM_LIMIT)


def _sigmoid(x):
    return 1.0 / (1.0 + jnp.exp(-x))


def _block_diag_ones(block):
    r = lax.broadcasted_iota(jnp.int32, (MXU_DIM, MXU_DIM), 0)
    c = lax.broadcasted_iota(jnp.int32, (MXU_DIM, MXU_DIM), 1)
    sh = int(np.log2(block))
    return jnp.where((r >> sh) == (c >> sh), 1.0, 0.0).astype(BF16)


def _group_sum(x, bd):
    outs = []
    for c in range(x.shape[1] // MXU_DIM):
        xs = x[:, c * MXU_DIM:(c + 1) * MXU_DIM]
        hi = xs.astype(BF16)
        r1 = xs - hi.astype(F32)
        mid = r1.astype(BF16)
        lo = (r1 - mid.astype(F32)).astype(BF16)
        acc = jnp.dot(hi, bd, preferred_element_type=F32)
        acc += jnp.dot(mid, bd, preferred_element_type=F32)
        acc += jnp.dot(lo, bd, preferred_element_type=F32)
        outs.append(acc)
    return jnp.concatenate(outs, axis=1)


def _rms_mod(x, g, mod, shift_row, scale_row):
    ms = jnp.mean(x * x, axis=-1, keepdims=True)
    y = x * lax.rsqrt(ms + NORM_EPS) * g
    return y * (1.0 + mod[scale_row:scale_row + 1, :]) + mod[shift_row:shift_row + 1, :]


def _ada_kernel(s_ref, w_ref, b_ref, o_ref):
    s = s_ref[...]
    s = s * _sigmoid(s)
    o_ref[...] = jnp.dot(s, w_ref[...], precision=HIGHEST, preferred_element_type=F32) + b_ref[...]


def ada_mod(cond, w_ada, b_ada):
    depth, d, n = w_ada.shape
    tn = n // 4
    return pl.pallas_call(
        _ada_kernel,
        grid=(depth, n // tn),
        in_specs=[pl.BlockSpec((MOD_ROWS, d), lambda l, j: (0, 0)),
                  pl.BlockSpec((None, d, tn), lambda l, j: (l, 0, j)),
                  pl.BlockSpec((None, 1, tn), lambda l, j: (l, 0, j))],
        out_specs=pl.BlockSpec((None, MOD_ROWS, tn), lambda l, j: (l, 0, j)),
        out_shape=jax.ShapeDtypeStruct((depth, MOD_ROWS, n), F32),
        compiler_params=_cparams(("arbitrary", "arbitrary")),
        name="ada_mod",
    )(cond, w_ada, b_ada.reshape(depth, 1, n))


def _norm_mod_kernel(x_ref, g_ref, mod_ref, hb_ref, hf_ref):
    h = _rms_mod(x_ref[...], g_ref[...], mod_ref[...], SH1, SC1)
    hb_ref[...] = h.astype(BF16)
    hf_ref[...] = h


def norm_mod(x_all, g, modtab):
    b, tt, d = x_all.shape
    nt = tt // ROW_TILE
    ctx_tile = nt - 1
    spec = pl.BlockSpec((None, ROW_TILE, d), lambda bi, i: (bi, i, 0))
    return pl.pallas_call(
        _norm_mod_kernel,
        grid=(b, nt),
        in_specs=[spec,
                  pl.BlockSpec((1, d), lambda bi, i: (0, 0)),
                  pl.BlockSpec((None, None, MOD_ROWS, d), lambda bi, i: (bi, jnp.where(i == ctx_tile, 0, 1), 0, 0))],
        out_specs=[spec, spec],
        out_shape=[jax.ShapeDtypeStruct((b, tt, d), BF16), jax.ShapeDtypeStruct((b, tt, d), F32)],
        compiler_params=_cparams(("parallel", "parallel")),
        name="norm_mod",
    )(x_all, g.reshape(1, d), modtab)


def _mm_kernel(a_ref, w_ref, o_ref, *, precision):
    o_ref[...] = jnp.dot(a_ref[...], w_ref[...], precision=precision,
                         preferred_element_type=F32).astype(o_ref.dtype)


def matmul(a, w, tm, tn, out_dtype=F32, precision=None):
    m, k = a.shape
    n = w.shape[1]
    return pl.pallas_call(
        functools.partial(_mm_kernel, precision=precision),
        grid=(n // tn, m // tm),
        in_specs=[pl.BlockSpec((tm, k), lambda j, i: (i, 0)),
                  pl.BlockSpec((k, tn), lambda j, i: (0, j))],
        out_specs=pl.BlockSpec((tm, tn), lambda j, i: (i, j)),
        out_shape=jax.ShapeDtypeStruct((m, n), out_dtype),
        compiler_params=_cparams(("parallel", "parallel")),
        name="matmul",
    )(a, w)


def _mlstm_kernel(qf, kf, vf, gif, gff, qb, kb, vb, gib, gfb, bi_ref, bf_ref, of, ob, ct_ref, m_ref):
    @pl.when(pl.program_id(1) == 0)
    def _():
        ct_ref[...] = jnp.zeros_like(ct_ref)
        m_ref[...] = jnp.zeros_like(m_ref)

    gates = (_mlstm_gates(gif, gff, bi_ref, bf_ref, m_ref.at[0], False),
             _mlstm_gates(gib, gfb, bi_ref, bf_ref, m_ref.at[1], True))
    refs = ((qf, kf, vf, of), (qb, kb, vb, ob))
    for h in range(MLSTM_HEADS):
        for d in range(2):
            _mlstm_head(h, *refs[d], ct_ref.at[d], gates[d], bool(d))


def _running_max(x, rev):
    n = x.shape[0]
    rowi = lax.broadcasted_iota(jnp.int32, (n, 1), 0)
    sh = 1
    while sh < n:
        if rev:
            x = jnp.maximum(x, jnp.where(rowi < n - sh, pltpu.roll(x, n - sh, 0), -jnp.inf))
        else:
            x = jnp.maximum(x, jnp.where(rowi >= sh, pltpu.roll(x, sh, 0), -jnp.inf))
        sh *= 2
    return x


def _mlstm_gates(gi_ref, gf_ref, bi_ref, bf_ref, m_ref, rev):
    L = ROW_TILE

    def cap(a):
        return GATE_SOFT_CAP * jnp.tanh(a / GATE_SOFT_CAP)

    ic = cap(gi_ref[...] + bi_ref[...])
    fp = cap(gf_ref[...] + bf_ref[...])
    lf = jnp.minimum(fp, 0.0) - jnp.log(1.0 + jnp.exp(-jnp.abs(fp)))
    row = lax.broadcasted_iota(jnp.int32, (L, L), 0)
    col = lax.broadcasted_iota(jnp.int32, (L, L), 1)
    tri = (col >= row) if rev else (col <= row)
    b = jnp.dot(jnp.where(tri, 1.0, 0.0), lf, precision=HIGHEST, preferred_element_type=F32)
    src = ic - b
    m_prev = m_ref[0:1, :]
    mu = jnp.maximum(m_prev, _running_max(src, rev))
    w_inter = jnp.exp(m_prev - mu)
    e_den = jnp.exp(-(b + mu))
    last = 0 if rev else L - 1
    b_last = b[last:last + 1, :]
    w_src = b_last - b + ic
    m_new = jnp.maximum(b_last + m_prev, jnp.max(w_src, axis=0, keepdims=True))
    a_src = jnp.exp(w_src - m_new)
    a_old = jnp.exp(b_last + m_prev - m_new)
    m_ref[...] = jnp.broadcast_to(m_new, m_ref.shape)
    return src.T, mu, w_inter, e_den, a_src, a_old, tri


def _mlstm_head(h, q_ref, k_ref, v_ref, o_ref, ct_ref, gates, rev):
    L = ROW_TILE
    H, DQK, DV = MLSTM_HEADS, MLSTM_DQK, MLSTM_DV
    src_t, mu_all, w_inter_all, e_den_all, a_src_all, a_old_all, tri = gates
    lane0 = lax.broadcasted_iota(jnp.int32, (L, LANES), 1) == 0
    ones_blk = jnp.where(lane0, 1.0, 0.0).astype(BF16)
    scale = DQK ** -0.5
    ln = (H if rev else 0) + h
    col = lambda a: a[:, ln:ln + 1]
    w_intra = jnp.exp(jnp.where(tri, src_t[ln:ln + 1, :] - col(mu_all), -jnp.inf))
    w_inter = col(w_inter_all)
    qb = (q_ref[:, h * DQK:(h + 1) * DQK] * scale).astype(BF16)
    kf = k_ref[:, h * DQK:(h + 1) * DQK]
    kb = kf.astype(BF16)
    vaug = jnp.concatenate([v_ref[:, h * DV:(h + 1) * DV].astype(BF16), ones_blk], axis=1)
    sqk = lax.dot_general(qb, kb, (((1,), (1,)), ((), ())), preferred_element_type=F32)
    sw = (sqk * w_intra).astype(BF16)
    ct = ct_ref[h]
    num_aug = (jnp.dot(sw, vaug, preferred_element_type=F32)
               + w_inter * jnp.dot(qb, ct.astype(BF16), preferred_element_type=F32))
    num = num_aug[:, :DV]
    den = num_aug[:, DV:DV + 1]
    o_ref[:, h * DV:(h + 1) * DV] = num / jnp.maximum(jnp.abs(den), col(e_den_all))
    ks = (kf * col(a_src_all)).astype(BF16)
    upd = lax.dot_general(ks, vaug, (((0,), (0,)), ((), ())), preferred_element_type=F32)
    ct_ref[h] = col(a_old_all) * ct + upd


def mlstm_scan(qkvo, gates, bias_i, bias_f):
    b, tt, _ = qkvo.shape
    nt = tt // ROW_TILE
    ctx_tile = nt - 1
    hq = MLSTM_HEADS * MLSTM_DQK
    hv = MLSTM_HEADS * MLSTM_DV
    order_f = lambda s: jnp.where(s == 0, ctx_tile, s - 1)
    order_b = lambda s: jnp.where(s == 0, ctx_tile, ctx_tile - s)

    def operands(order):
        return [pl.BlockSpec((None, ROW_TILE, hq), lambda bi, s: (bi, order(s), 0)),
                pl.BlockSpec((None, ROW_TILE, hq), lambda bi, s: (bi, order(s), 1)),
                pl.BlockSpec((None, ROW_TILE, hv), lambda bi, s: (bi, order(s), 1)),
                pl.BlockSpec((None, ROW_TILE, LANES), lambda bi, s: (bi, order(s), 0)),
                pl.BlockSpec((None, ROW_TILE, LANES), lambda bi, s: (bi, order(s), 1))]

    bias = pl.BlockSpec((1, LANES), lambda bi, s: (0, 0))
    return pl.pallas_call(
        _mlstm_kernel,
        grid=(b, nt),
        in_specs=operands(order_f) + operands(order_b) + [bias, bias],
        out_specs=[pl.BlockSpec((None, ROW_TILE, hv), lambda bi, s: (bi, order_f(s), 0)),
                   pl.BlockSpec((None, ROW_TILE, hv), lambda bi, s: (bi, order_b(s), 0))],
        out_shape=[jax.ShapeDtypeStruct((b, tt, hv), F32)] * 2,
        scratch_shapes=[pltpu.VMEM((2, MLSTM_HEADS, MLSTM_DQK, 2 * MLSTM_DV), F32),
                        pltpu.VMEM((2, 8, LANES), F32)],
        compiler_params=_cparams(("parallel", "arbitrary")),
        name="mlstm",
    )(*([qkvo] * 3 + [gates] * 2) * 2, bias_i, bias_f)


OP_KK, OP_R, OP_WF, OP_BF, OP_KF, OP_WB, OP_BB, OP_KB = range(8)
N_OPS = 8


def _rwkv_prep_kernel(p_ref, hp_ref, hn_ref, mu_ref, w0_ref, w2_ref, a0_ref, a2_ref, g2_ref, kk_ref, ka_ref,
                      rk_ref, ops_o, v_o, g_o, bonus_o, xs_ref):
    T = ROW_TILE
    i = pl.program_id(1)
    n_lat = pl.num_programs(1) - 1
    x = p_ref[...]
    W = x.shape[1]
    rowi = lax.broadcasted_iota(jnp.int32, (T, 1), 0)
    coli = lax.broadcasted_iota(jnp.int32, (1, W), 1)
    prev = pltpu.roll(x, 1, 0)
    nxt = pltpu.roll(x, T - 1, 0)

    @pl.when(i < n_lat)
    def _():
        gc = rowi & (GRID_W - 1)
        left = jnp.where(gc == 0, 0.0, prev)
        right = jnp.where(gc == GRID_W - 1, 0.0, nxt)
        hp = jnp.where(i == 0, 0.0, hp_ref[...])
        hn = jnp.where(i == n_lat - 1, 0.0, hn_ref[...])
        up = jnp.concatenate([hp, x[:T - GRID_W]], axis=0)
        down = jnp.concatenate([x[GRID_W:], hn], axis=0)
        q = W // 4
        xs_ref[...] = jnp.where(coli < q, left, jnp.where(coli < 2 * q, right, jnp.where(coli < 3 * q, up, down)))

    @pl.when(i == n_lat)
    def _():
        xs_ref[...] = jnp.where(coli < W // 2, jnp.where(rowi == 0, 0.0, prev), jnp.where(rowi == T - 1, 0.0, nxt))

    xm = x + (xs_ref[...] - x) * mu_ref[...]
    C = RWKV_HEADS * RWKV_N
    r = xm[:, 0:C]
    k = xm[:, C:2 * C]
    v = xm[:, 2 * C:3 * C]
    wd = jnp.tanh(xm[:, 3 * C:3 * C + 2 * DECAY_LORA]).astype(BF16)
    ad = xm[:, 3 * C + 2 * DECAY_LORA:3 * C + 2 * DECAY_LORA + 2 * AAA_LORA].astype(BF16)
    gd = _sigmoid(xm[:, 3 * C + 2 * DECAY_LORA + 2 * AAA_LORA:]).astype(BF16)
    bd = _block_diag_ones(RWKV_N)

    kk = k * kk_ref[...]
    nrm = jnp.sqrt(_group_sum(kk * kk, bd))
    kk = kk / jnp.maximum(nrm, 1e-12)
    u = w0_ref[...] + jnp.dot(wd, w2_ref[...], preferred_element_type=F32)
    decay = jnp.exp(-np.float32(np.exp(-0.5)) * _sigmoid(u))
    a = _sigmoid(a0_ref[...] + jnp.dot(ad, a2_ref[...], preferred_element_type=F32))
    ka = ka_ref[...]
    kd_f = k * (1.0 + (a[:, :C] - 1.0) * ka)
    kd_b = k * (1.0 + (a[:, C:] - 1.0) * ka)
    ops_o[OP_R] = r.T
    v_o[...] = v.T
    ops_o[OP_KK] = kk.T
    ops_o[OP_WF] = decay[:, :C].T
    ops_o[OP_WB] = decay[:, C:].T
    ops_o[OP_BF] = (kk * a[:, :C]).T
    ops_o[OP_BB] = (kk * a[:, C:]).T
    ops_o[OP_KF] = kd_f.T
    ops_o[OP_KB] = kd_b.T
    g_o[...] = jnp.dot(gd, g2_ref[...], preferred_element_type=F32)
    bonus_o[...] = _group_sum(r * (kd_f + kd_b) * rk_ref[...], bd) * v


def rwkv_prepare(p, mu, w0cat, w2cat, a0cat, a2cat, g2, k_k, k_a, r_k):
    b, tt, w = p.shape
    nt = tt // ROW_TILE
    n_lat = nt - 1
    hpt = ROW_TILE // GRID_W
    c = RWKV_HEADS * RWKV_N
    n_halo = tt // GRID_W
    const = lambda shape: pl.BlockSpec(shape, lambda bi, i: (0,) * len(shape))
    out_spec = pl.BlockSpec((None, ROW_TILE, c), lambda bi, i: (bi, i, 0))
    return pl.pallas_call(
        _rwkv_prep_kernel,
        grid=(b, nt),
        in_specs=[pl.BlockSpec((None, ROW_TILE, w), lambda bi, i: (bi, i, 0)),
                  pl.BlockSpec((None, GRID_W, w), lambda bi, i: (bi, jnp.maximum(i * hpt - 1, 0), 0)),
                  pl.BlockSpec((None, GRID_W, w), lambda bi, i: (bi, jnp.minimum(i * hpt + hpt, n_halo - 1), 0)),
                  const((1, w)), const((1, 2 * c)), const((2 * DECAY_LORA, 2 * c)), const((1, 2 * c)),
                  const((2 * AAA_LORA, 2 * c)), const((GATE_LORA, c)), const((1, c)), const((1, c)), const((1, c))],
        out_specs=[pl.BlockSpec((N_OPS, None, c, ROW_TILE), lambda bi, i: (0, bi, 0, i)),
                   pl.BlockSpec((None, c, ROW_TILE), lambda bi, i: (bi, 0, i)), out_spec, out_spec],
        out_shape=[jax.ShapeDtypeStruct((N_OPS, b, c, tt), F32), jax.ShapeDtypeStruct((b, c, tt), F32),
                   jax.ShapeDtypeStruct((b, tt, c), F32), jax.ShapeDtypeStruct((b, tt, c), F32)],
        scratch_shapes=[pltpu.VMEM((ROW_TILE, w), F32)],
        compiler_params=_cparams(("parallel", "parallel")),
        name="rwkv_prepare",
    )(p, p, p, mu, w0cat, w2cat, a0cat, a2cat, g2, k_k, k_a, r_k)


RELAYOUT_ROWS = 128
V_HALF = RWKV_N // 2


def _to_scan_k_kernel(x_ref, o_ref):
    nbh = x_ref.shape[0] // RWKV_N
    for k in range(RWKV_N):
        g = x_ref[pl.ds(k, nbh, stride=RWKV_N), :]
        o_ref[k] = jnp.concatenate([g, g], axis=0).T


def to_scan_k(ops_t):
    n_ops, b, c, tt = ops_t.shape
    return pl.pallas_call(
        _to_scan_k_kernel,
        grid=(n_ops, tt // RELAYOUT_ROWS),
        in_specs=[pl.BlockSpec((None, b * c, RELAYOUT_ROWS), lambda o, i: (o, 0, i))],
        out_specs=pl.BlockSpec((None, RWKV_N, RELAYOUT_ROWS, LANES), lambda o, i: (o, 0, i, 0)),
        out_shape=jax.ShapeDtypeStruct((n_ops, RWKV_N, tt, LANES), F32),
        compiler_params=_cparams(("parallel", "parallel")),
        name="to_scan_k",
    )(ops_t.reshape(n_ops, b * c, tt))


def _to_scan_v_kernel(x_ref, o_ref):
    nbh = x_ref.shape[0] // RWKV_N
    for v in range(V_HALF):
        m = jnp.concatenate([x_ref[pl.ds(v, nbh, stride=RWKV_N), :],
                             x_ref[pl.ds(V_HALF + v, nbh, stride=RWKV_N), :]], axis=0)
        o_ref[:, v, :] = m.T


def to_scan_v(x_t):
    b, c, tt = x_t.shape
    return pl.pallas_call(
        _to_scan_v_kernel,
        grid=(tt // RELAYOUT_ROWS,),
        in_specs=[pl.BlockSpec((b * c, RELAYOUT_ROWS), lambda i: (0, i))],
        out_specs=pl.BlockSpec((RELAYOUT_ROWS, V_HALF, LANES), lambda i: (i, 0, 0)),
        out_shape=jax.ShapeDtypeStruct((tt, V_HALF, LANES), F32),
        compiler_params=_cparams(("parallel",)),
        name="to_scan_v",
    )(x_t.reshape(b * c, tt))


def _from_scan_kernel(yf_ref, yb_ref, o_ref, zs_ref):
    nbh = zs_ref.shape[0] // RWKV_N
    for v in range(V_HALF):
        t = (yf_ref[:, v, :] + yb_ref[:, v, :]).T
        zs_ref[pl.ds(v, nbh, stride=RWKV_N), :] = t[:nbh]
        zs_ref[pl.ds(V_HALF + v, nbh, stride=RWKV_N), :] = t[nbh:]
    for b in range(o_ref.shape[0]):
        for cb in range(o_ref.shape[2] // LANES):
            r0 = (b * (o_ref.shape[2] // LANES) + cb) * LANES
            o_ref[b, :, cb * LANES:(cb + 1) * LANES] = zs_ref[r0:r0 + LANES, :].T


def from_scan(yf, yb, b):
    tt = yf.shape[0]
    c = RWKV_HEADS * RWKV_N
    spec = pl.BlockSpec((RELAYOUT_ROWS, V_HALF, LANES), lambda i: (i, 0, 0))
    return pl.pallas_call(
        _from_scan_kernel,
        grid=(tt // RELAYOUT_ROWS,),
        in_specs=[spec, spec],
        out_specs=pl.BlockSpec((b, RELAYOUT_ROWS, c), lambda i: (0, i, 0)),
        out_shape=jax.ShapeDtypeStruct((b, tt, c), F32),
        scratch_shapes=[pltpu.VMEM((b * c, RELAYOUT_ROWS), F32)],
        compiler_params=_cparams(("parallel",)),
        name="from_scan",
    )(yf, yb)


SCAN_ROWS = 32


def _rwkv_scan_kernel(kk_f, kkn_f, r_f, w_f, b_f, k_f, kk_b, kkn_b, r_b, w_b, b_b, k_b, v_f, v_b, y_f, y_b,
                      s_ref, sa_ref, kkx_f, kkx_b):
    N, TB, PAD = RWKV_N, SCAN_ROWS, 8

    @pl.when(pl.program_id(0) == 0)
    def _():
        s_ref[...] = jnp.zeros_like(s_ref)
        sa_ref[...] = jnp.zeros_like(sa_ref)

    kkx_f[:, 0:TB, :] = kk_f[...]
    kkx_f[:, TB:TB + PAD, :] = kkn_f[:, 0:PAD, :]
    kkx_b[:, PAD:PAD + TB, :] = kk_b[...]
    kkx_b[:, 0:PAD, :] = kkn_b[:, TB - PAD:TB, :]

    dirs = ((kkx_f, r_f, w_f, b_f, k_f, v_f, y_f), (kkx_b, r_b, w_b, b_b, k_b, v_b, y_b))

    def step(j, carry):
        for d, (kkx_r, r_r, w_r, b_r, k_r, v_r, y_r) in enumerate(dirs):
            row, nxt = (j, j + 1) if d == 0 else (TB - 1 - j, PAD + TB - 2 - j)
            op = lambda ref, kx, rw: ref[kx, pl.ds(rw, 1), :]
            sa = sa_ref[d]
            vt = v_r[row]
            acc_sa = [jnp.zeros((V_HALF, LANES), F32), jnp.zeros((V_HALF, LANES), F32)]
            acc_y = [jnp.zeros((V_HALF, LANES), F32), jnp.zeros((V_HALF, LANES), F32)]
            for kx in range(N):
                sk = s_ref[d, kx] * op(w_r, kx, row) - sa * op(b_r, kx, row) + vt * op(k_r, kx, row)
                s_ref[d, kx] = sk
                acc_sa[kx % 2] = acc_sa[kx % 2] + sk * op(kkx_r, kx, nxt)
                acc_y[kx % 2] = acc_y[kx % 2] + sk * op(r_r, kx, row)
            sa_ref[d] = acc_sa[0] + acc_sa[1]
            y_r[row] = acc_y[0] + acc_y[1]
        return carry

    lax.fori_loop(0, SCAN_ROWS, step, 0)


def rwkv_scan(ops_s, v_s, seq):
    _, n, tt, lanes = ops_s.shape
    n_lat = seq // SCAN_ROWS
    n_blk = tt // SCAN_ROWS
    n_ctx = n_blk - n_lat
    blk_f = lambda s: jnp.where(s < n_ctx, n_lat + s, s - n_ctx)
    blk_b = lambda s: n_blk - 1 - s
    kspec = lambda o, blk: pl.BlockSpec((None, n, SCAN_ROWS, lanes), lambda s: (o, 0, blk(s), 0))
    vspec = lambda blk: pl.BlockSpec((SCAN_ROWS, V_HALF, lanes), lambda s: (blk(s), 0, 0))
    nxt = lambda blk: (lambda s: blk(jnp.minimum(s + 1, n_blk - 1)))
    fwd = [kspec(OP_KK, blk_f), kspec(OP_KK, nxt(blk_f))] + [kspec(o, blk_f) for o in (OP_R, OP_WF, OP_BF, OP_KF)]
    bwd = [kspec(OP_KK, blk_b), kspec(OP_KK, nxt(blk_b))] + [kspec(o, blk_b) for o in (OP_R, OP_WB, OP_BB, OP_KB)]
    return pl.pallas_call(
        _rwkv_scan_kernel,
        grid=(n_blk,),
        in_specs=fwd + bwd + [vspec(blk_f), vspec(blk_b)],
        out_specs=[vspec(blk_f), vspec(blk_b)],
        out_shape=[jax.ShapeDtypeStruct((tt, V_HALF, lanes), F32)] * 2,
        scratch_shapes=[pltpu.VMEM((2, n, V_HALF, lanes), F32), pltpu.VMEM((2, V_HALF, lanes), F32),
                        pltpu.VMEM((n, SCAN_ROWS + 8, lanes), F32), pltpu.VMEM((n, SCAN_ROWS + 8, lanes), F32)],
        compiler_params=_cparams(("arbitrary",)),
        name="rwkv_scan",
    )(*([ops_s] * 12), v_s, v_s)


def _merge_kernel(hf_ref, hb_ref, o_ref, y_ref, bonus_ref, g_ref, mg_ref, x_ref, mod_ref, gm_ref, lnw_ref,
                  lnb_ref, wpm_ref, wpr_ref, wout_ref, out_ref):
    c = RWKV_HEADS * RWKV_N
    hm = hf_ref[...] + hb_ref[...]
    ms = _group_sum(hm * hm, _block_diag_ones(MLSTM_DV)) * (1.0 / MLSTM_DV)
    hmn = hm * lax.rsqrt(ms + NORM_EPS) * gm_ref[...] * _sigmoid(o_ref[...])
    ym = jnp.dot(hmn.astype(BF16), wpm_ref[...], preferred_element_type=F32)
    bd = _block_diag_ones(RWKV_N)
    y = y_ref[...]
    yc = y - _group_sum(y, bd) * (1.0 / RWKV_N)
    var = _group_sum(yc * yc, bd) * (1.0 / RWKV_N)
    yn = yc * lax.rsqrt(var + GN_EPS) * lnw_ref[...] + lnb_ref[...]
    yr_in = ((yn + bonus_ref[...]) * g_ref[...]).astype(BF16)
    yr = jnp.dot(yr_in, wpr_ref[...], preferred_element_type=F32)
    mg = mg_ref[...]
    z = _sigmoid(mg[:, :c]) * ym + _sigmoid(mg[:, c:]) * yr
    yy = jnp.dot(z.astype(BF16), wout_ref[...], preferred_element_type=F32)
    out_ref[...] = x_ref[...] + mod_ref[GT1:GT1 + 1, :] * yy


def merge(hf, hb, qkvo, y, bonus, g, mg, x_all, modtab, gm, lnw, lnb, wpm, wpr, wout, nt_out):
    b, tt, d = x_all.shape
    ctx_tile = tt // ROW_TILE - 1
    row = lambda width, colblk=0: pl.BlockSpec((None, ROW_TILE, width), lambda bi, i: (bi, i, colblk))
    const = lambda shape: pl.BlockSpec(shape, lambda bi, i: (0,) * len(shape))
    return pl.pallas_call(
        _merge_kernel,
        grid=(b, nt_out),
        in_specs=[row(d), row(d), row(d, 2), row(d), row(d), row(d), row(2 * d), row(d),
                  pl.BlockSpec((None, None, MOD_ROWS, d), lambda bi, i: (bi, jnp.where(i == ctx_tile, 0, 1), 0, 0)),
                  const((1, d)), const((1, d)), const((1, d)), const((d, d)), const((d, d)), const((d, d))],
        out_specs=row(d),
        out_shape=jax.ShapeDtypeStruct((b, nt_out * ROW_TILE, d), F32),
        compiler_params=_cparams(("parallel", "parallel")),
        name="merge",
    )(hf, hb, qkvo, y, bonus, g, mg, x_all, modtab, gm, lnw, lnb, wpm, wpr, wout)


def _ffn_kernel(x_ref, g_ref, mod_ref, wg_ref, wu_ref, wd_ref, o_ref, h_scr, acc_scr):
    j = pl.program_id(2)

    @pl.when(j == 0)
    def _():
        h_scr[...] = _rms_mod(x_ref[...], g_ref[...], mod_ref[...], SH2, SC2).astype(BF16)
        acc_scr[...] = jnp.zeros_like(acc_scr)

    h = h_scr[...]
    a = jnp.dot(h, wg_ref[...], preferred_element_type=F32)
    u = jnp.dot(h, wu_ref[...], preferred_element_type=F32)
    t = (a * _sigmoid(a) * u).astype(BF16)
    acc_scr[...] += jnp.dot(t, wd_ref[...], preferred_element_type=F32)

    @pl.when(j == pl.num_programs(2) - 1)
    def _():
        o_ref[...] = x_ref[...] + mod_ref[GT2:GT2 + 1, :] * acc_scr[...]


def ffn_dense(x_all, g, modtab, wg, wu, wd, tf):
    b, tt, d = x_all.shape
    nt = tt // ROW_TILE
    ctx_tile = nt - 1
    f = wg.shape[1]
    row = pl.BlockSpec((None, ROW_TILE, d), lambda bi, i, j: (bi, i, 0))
    return pl.pallas_call(
        _ffn_kernel,
        grid=(b, nt, f // tf),
        in_specs=[row,
                  pl.BlockSpec((1, d), lambda bi, i, j: (0, 0)),
                  pl.BlockSpec((None, None, MOD_ROWS, d),
                               lambda bi, i, j: (bi, jnp.where(i == ctx_tile, 0, 1), 0, 0)),
                  pl.BlockSpec((d, tf), lambda bi, i, j: (0, j)),
                  pl.BlockSpec((d, tf), lambda bi, i, j: (0, j)),
                  pl.BlockSpec((tf, d), lambda bi, i, j: (j, 0))],
        out_specs=row,
        out_shape=jax.ShapeDtypeStruct((b, tt, d), F32),
        scratch_shapes=[pltpu.VMEM((ROW_TILE, d), BF16), pltpu.VMEM((ROW_TILE, d), F32)],
        compiler_params=_cparams(("parallel", "parallel", "arbitrary")),
        name="ffn_dense",
    )(x_all, g.reshape(1, d), modtab, wg, wu, wd)


MOE_ROWS = 512


TOP_K = 2
MOE_TOKENS = 256
SEL_E1, SEL_E2, SEL_P1, SEL_P2 = range(4)


def _moe_route_kernel(x_ref, g_ref, mod_ref, wr_ref, h_o, sel_o):
    h = _rms_mod(x_ref[...], g_ref[...], mod_ref[...], SH2, SC2)
    h_o[...] = h
    logits = jnp.dot(h, wr_ref[...], precision=HIGHEST, preferred_element_type=F32)
    lane = lax.broadcasted_iota(jnp.int32, logits.shape, 1)
    logits = jnp.where(lane < N_EXPERTS, logits, -jnp.inf)
    m1 = jnp.max(logits, axis=-1, keepdims=True)
    i1 = jnp.min(jnp.where(logits == m1, lane, LANES), axis=-1, keepdims=True)
    rest = jnp.where(lane == i1, -jnp.inf, logits)
    m2 = jnp.max(rest, axis=-1, keepdims=True)
    i2 = jnp.min(jnp.where(rest == m2, lane, LANES), axis=-1, keepdims=True)
    e21 = jnp.exp(m2 - m1)
    p1 = 1.0 / (1.0 + e21)
    sel_o[...] = jnp.where(lane == SEL_E1, i1.astype(F32),
                           jnp.where(lane == SEL_E2, i2.astype(F32),
                                     jnp.where(lane == SEL_P1, p1, jnp.where(lane == SEL_P2, e21 * p1, 0.0))))


def moe_route(x_lat, g, modtab, w_router):
    b, s, d = x_lat.shape
    nt = s // MOE_ROWS
    return pl.pallas_call(
        _moe_route_kernel,
        grid=(b, nt),
        in_specs=[pl.BlockSpec((None, MOE_ROWS, d), lambda bi, i: (bi, i, 0)),
                  pl.BlockSpec((1, d), lambda bi, i: (0, 0)),
                  pl.BlockSpec((None, None, MOD_ROWS, d), lambda bi, i: (bi, 1, 0, 0)),
                  pl.BlockSpec((d, LANES), lambda bi, i: (0, 0))],
        out_specs=[pl.BlockSpec((MOE_ROWS, d), lambda bi, i: (bi * nt + i, 0)),
                   pl.BlockSpec((MOE_ROWS, LANES), lambda bi, i: (bi * nt + i, 0))],
        out_shape=[jax.ShapeDtypeStruct((b * s, d), F32), jax.ShapeDtypeStruct((b * s, LANES), F32)],
        compiler_params=_cparams(("parallel", "parallel")),
        name="moe_route",
    )(x_lat, g.reshape(1, d), modtab, w_router)


def _row_copy(src, src_row, dst, dst_row, sem):
    return pltpu.make_async_copy(src.at[pl.ds(src_row, 1)], dst.at[pl.ds(dst_row, 1)], sem)


ORDER_CHUNK = 1024


def _moe_dispatch_kernel(s0_ref, ord_lo, ord_hi, h_hbm, xg_ref, sem, *, n_choices):
    s0 = s0_ref[pl.program_id(0)]
    base = s0 - (s0 & (ORDER_CHUNK - 1))

    def start(r, c):
        j = jnp.minimum(s0 + r, n_choices - 1) - base
        choice = jnp.where(j < ORDER_CHUNK, ord_lo[jnp.minimum(j, ORDER_CHUNK - 1)],
                           ord_hi[jnp.maximum(j - ORDER_CHUNK, 0)])
        _row_copy(h_hbm, lax.shift_right_logical(choice, 1), xg_ref, r, sem).start()
        return c

    def wait(r, c):
        _row_copy(h_hbm, 0, xg_ref, 0, sem).wait()
        return c

    lax.fori_loop(0, MOE_TOKENS, start, 0, unroll=8)
    lax.fori_loop(0, MOE_TOKENS, wait, 0, unroll=8)


def moe_dispatch(s0, order, h, rows):
    d = h.shape[1]
    n_chunks = order.shape[0] // ORDER_CHUNK
    return pl.pallas_call(
        functools.partial(_moe_dispatch_kernel, n_choices=order.shape[0]),
        grid_spec=pltpu.PrefetchScalarGridSpec(
            num_scalar_prefetch=1,
            grid=(rows // MOE_TOKENS,),
            in_specs=[pl.BlockSpec((ORDER_CHUNK,), lambda i, s: (s[i] // ORDER_CHUNK,), memory_space=pltpu.SMEM),
                      pl.BlockSpec((ORDER_CHUNK,), lambda i, s: (jnp.minimum(s[i] // ORDER_CHUNK + 1, n_chunks - 1),),
                                   memory_space=pltpu.SMEM),
                      pl.BlockSpec(memory_space=pl.ANY)],
            out_specs=pl.BlockSpec((MOE_TOKENS, d), lambda i, s: (i, 0)),
            scratch_shapes=[pltpu.SemaphoreType.DMA(())]),
        out_shape=jax.ShapeDtypeStruct((rows, d), F32),
        compiler_params=_cparams(("arbitrary",)),
        name="moe_dispatch",
    )(s0, order, order, h)


def _moe_experts_kernel(te_ref, nu_ref, xg_ref, wg_ref, wu_ref, wd_ref, y_ref, h_scr, acc_scr):
    del te_ref
    i = pl.program_id(0)
    j = pl.program_id(1)
    used = i < nu_ref[0]
    last = j == pl.num_programs(1) - 1

    @pl.when(jnp.logical_and(used, j == 0))
    def _():
        h_scr[...] = xg_ref[...].astype(BF16)
        acc_scr[...] = jnp.zeros_like(acc_scr)

    @pl.when(used)
    def _():
        h = h_scr[...]
        a = jnp.dot(h, wg_ref[...], preferred_element_type=F32)
        u = jnp.dot(h, wu_ref[...], preferred_element_type=F32)
        t = (a * _sigmoid(a) * u).astype(BF16)
        acc_scr[...] += jnp.dot(t, wd_ref[...], preferred_element_type=F32)

    @pl.when(jnp.logical_and(used, last))
    def _():
        y_ref[...] = acc_scr[...]

    @pl.when(jnp.logical_and(jnp.logical_not(used), last))
    def _():
        y_ref[...] = jnp.zeros_like(y_ref)


def moe_experts(tile_expert, n_used, xg, wg, wu, wd, tf):
    rows, d = xg.shape
    f = wg.shape[2]
    return pl.pallas_call(
        _moe_experts_kernel,
        grid_spec=pltpu.PrefetchScalarGridSpec(
            num_scalar_prefetch=2,
            grid=(rows // MOE_ROWS, f // tf),
            in_specs=[pl.BlockSpec((MOE_ROWS, d), lambda i, j, te, nu: (i, 0)),
                      pl.BlockSpec((None, d, tf), lambda i, j, te, nu: (te[i], 0, j)),
                      pl.BlockSpec((None, d, tf), lambda i, j, te, nu: (te[i], 0, j)),
                      pl.BlockSpec((None, tf, d), lambda i, j, te, nu: (te[i], j, 0))],
            out_specs=pl.BlockSpec((MOE_ROWS, d), lambda i, j, te, nu: (i, 0)),
            scratch_shapes=[pltpu.VMEM((MOE_ROWS, d), BF16), pltpu.VMEM((MOE_ROWS, d), F32)]),
        out_shape=jax.ShapeDtypeStruct((rows, d), F32),
        compiler_params=_cparams(("arbitrary", "arbitrary")),
        name="moe_experts",
    )(tile_expert, n_used, xg, wg, wu, wd)


def _moe_combine_kernel(pos_ref, sel_ref, x_ref, mod_ref, gfin_ref, y_hbm, o_ref, ybuf, sem):
    n = pos_ref.shape[-1]

    def start(a, c):
        _row_copy(y_hbm, pos_ref[0, a], ybuf.at[a & 1], lax.shift_right_logical(a, 1), sem).start()
        return c

    def wait(a, c):
        _row_copy(y_hbm, 0, ybuf.at[0], 0, sem).wait()
        return c

    lax.fori_loop(0, n, start, 0, unroll=8)
    lax.fori_loop(0, n, wait, 0, unroll=8)
    sel = sel_ref[...]
    yy = sel[:, SEL_P1:SEL_P1 + 1] * ybuf[0] + sel[:, SEL_P2:SEL_P2 + 1] * ybuf[1]
    xo = x_ref[...] + mod_ref[GT2:GT2 + 1, :] * yy
    ms = jnp.mean(xo * xo, axis=-1, keepdims=True)
    o_ref[...] = xo * lax.rsqrt(ms + NORM_EPS) * gfin_ref[...]


def moe_combine(pos, sel, x_lat, modtab, g_final, y):
    b, s, d = x_lat.shape
    nt = s // MOE_TOKENS
    n = TOP_K * MOE_TOKENS
    row = pl.BlockSpec((None, MOE_TOKENS, d), lambda bi, i: (bi, i, 0))
    return pl.pallas_call(
        _moe_combine_kernel,
        grid=(b, nt),
        in_specs=[pl.BlockSpec((None, 1, n), lambda bi, i: (bi * nt + i, 0, 0), memory_space=pltpu.SMEM),
                  pl.BlockSpec((MOE_TOKENS, LANES), lambda bi, i: (bi * nt + i, 0)),
                  row,
                  pl.BlockSpec((None, None, MOD_ROWS, d), lambda bi, i: (bi, 1, 0, 0)),
                  pl.BlockSpec((1, d), lambda bi, i: (0, 0)),
                  pl.BlockSpec(memory_space=pl.ANY)],
        out_specs=row,
        out_shape=jax.ShapeDtypeStruct((b, s, d), F32),
        scratch_shapes=[pltpu.VMEM((TOP_K, MOE_TOKENS, d), F32), pltpu.SemaphoreType.DMA(())],
        compiler_params=_cparams(("arbitrary", "arbitrary")),
        name="moe_combine",
    )(pos.reshape(b * nt, 1, n), sel, x_lat, modtab, g_final.reshape(1, d), y)


def moe_final(x_lat, g, modtab, w_router, wg, wu, wd, g_final, tf):
    b, s, d = x_lat.shape
    m = b * s
    ne = wg.shape[0]
    h, sel = moe_route(x_lat, g, modtab, w_router)
    e = sel[:, SEL_E1:SEL_E2 + 1].astype(jnp.int32).reshape(m * TOP_K)
    onehot = (e[:, None] == jnp.arange(ne, dtype=jnp.int32)[None, :]).astype(jnp.int32)
    running = jnp.cumsum(onehot, axis=0)
    counts = running[-1]
    padded = (counts + MOE_ROWS - 1) // MOE_ROWS * MOE_ROWS
    group_end = jnp.cumsum(padded)
    group_start = group_end - padded
    pos = jnp.sum(onehot * (group_start[None, :] + running - 1), axis=1).astype(jnp.int32)
    n_tiles = m * TOP_K // MOE_ROWS + ne
    tile_expert = jnp.minimum(
        jnp.sum(jnp.arange(n_tiles, dtype=jnp.int32)[:, None] >= (group_end // MOE_ROWS)[None, :], axis=1),
        ne - 1).astype(jnp.int32)
    n_used = (group_end[-1:] // MOE_ROWS).astype(jnp.int32)
    order = jnp.argsort(e, stable=True).astype(jnp.int32)
    pad_before = group_start - (jnp.cumsum(counts) - counts)
    first_row = jnp.arange(n_tiles * MOE_ROWS // MOE_TOKENS, dtype=jnp.int32) * MOE_TOKENS
    s0 = jnp.clip(first_row - pad_before[tile_expert[first_row // MOE_ROWS]], 0,
                  m * TOP_K - 1).astype(jnp.int32)
    xg = moe_dispatch(s0, order, h, n_tiles * MOE_ROWS)
    y = moe_experts(tile_expert, n_used, xg, wg, wu, wd, tf)
    return moe_combine(pos, sel, x_lat, modtab, g_final, y)


def kernel(x, c, ctx, c_ctx, w_ada, b_ada, g_norm_mix, g_norm_ffn, w_in, b_mlstm_gate, g_mlstm_norm, mu_shift, w0, w2, a0, a2, g2, k_k, k_a, r_k, ln_w, ln_b, w_proj_mlstm, w_proj_rwkv, w_out, w_ff_gate, w_ff_up, w_ff_down, w_router, w_exp_gate, w_exp_up, w_exp_down, g_final):
    b, seq, d = x.shape
    ctx_len = ctx.shape[1]
    depth = w_ada.shape[0]
    assert ctx_len == ROW_TILE and seq % ROW_TILE == 0 and seq % MOE_ROWS == 0 and 2 * b * RWKV_HEADS == LANES
    tt = seq + ctx_len
    m_all = b * tt
    H = MLSTM_HEADS
    cr = RWKV_HEADS * RWKV_N
    hq, hv = H * MLSTM_DQK, H * MLSTM_DV

    x_all = jnp.concatenate([x, ctx], axis=1)
    cond = jnp.concatenate([c, c_ctx[None, :], jnp.zeros((MOD_ROWS - b - 1, d), F32)], axis=0)
    mod_all = ada_mod(cond, w_ada, b_ada).reshape(depth, MOD_ROWS, 6, d)

    o_gate = 2 * hq + 2 * hv
    o_rwkv = o_gate + 4 * H
    rwkv_w = 3 * cr + 2 * DECAY_LORA + 2 * AAA_LORA + GATE_LORA
    o_mg = o_rwkv + rwkv_w

    out = None
    for l in range(depth):
        need_ctx = l < depth - 1
        pad = jnp.zeros((b, MOD_ROWS - 6, d), F32)
        mod_lat = jnp.concatenate([mod_all[l, :b], pad], axis=1)
        mod_ctx = jnp.broadcast_to(jnp.concatenate([mod_all[l, b], pad[0]], axis=0), (b, MOD_ROWS, d))
        modtab = jnp.stack([mod_ctx, mod_lat], axis=1)

        wl = w_in[l]
        w_qkvo = wl[:, :o_gate].astype(BF16)
        wgt = wl[:, o_gate:o_rwkv].reshape(d, 4, H)
        zpad = jnp.zeros((d, LANES - 2 * H), F32)
        w_gates = jnp.concatenate([wgt[:, 0], wgt[:, 2], zpad, wgt[:, 1], wgt[:, 3], zpad], axis=1)
        bg = b_mlstm_gate[l]
        lpad = jnp.zeros((LANES - 2 * H,), F32)
        bias_i = jnp.concatenate([bg[0], bg[2], lpad]).reshape(1, LANES)
        bias_f = jnp.concatenate([bg[1], bg[3], lpad]).reshape(1, LANES)
        w_rwkv = wl[:, o_rwkv:o_mg].astype(BF16)
        w_mg = wl[:, o_mg:].astype(BF16)

        hb16, hf32 = norm_mod(x_all, g_norm_mix[l], modtab)
        hb16 = hb16.reshape(m_all, d)
        qkvo = matmul(hb16, w_qkvo, 512, 1024).reshape(b, tt, o_gate)
        gates = matmul(hf32.reshape(m_all, d), w_gates, 512, 2 * LANES, precision=HIGHEST).reshape(b, tt, 2 * LANES)
        prw = matmul(hb16, w_rwkv, 512, rwkv_w // 3).reshape(b, tt, rwkv_w)
        mg = matmul(hb16, w_mg, 512, 1024).reshape(b, tt, 2 * d)

        h_f, h_b = mlstm_scan(qkvo, gates, bias_i, bias_f)

        zl = jnp.zeros((DECAY_LORA, cr), F32)
        w2cat = jnp.concatenate([jnp.concatenate([w2[l, 0], zl], axis=1),
                                 jnp.concatenate([zl, w2[l, 1]], axis=1)], axis=0).astype(BF16)
        a2cat = jnp.concatenate([jnp.concatenate([a2[l, 0], zl], axis=1),
                                 jnp.concatenate([zl, a2[l, 1]], axis=1)], axis=0).astype(BF16)
        ops_, v_, g_, bonus_ = rwkv_prepare(
            prw, mu_shift[l].reshape(1, rwkv_w), w0[l].reshape(1, 2 * cr), w2cat, a0[l].reshape(1, 2 * cr), a2cat,
            g2[l].astype(BF16), k_k[l].reshape(1, cr), k_a[l].reshape(1, cr), r_k[l].reshape(1, cr))
        y_f, y_b = rwkv_scan(to_scan_k(ops_), to_scan_v(v_), seq)
        y_ = from_scan(y_f, y_b, b)

        nt_out = tt // ROW_TILE if need_ctx else seq // ROW_TILE
        x_mid = merge(h_f, h_b, qkvo, y_, bonus_, g_, mg, x_all, modtab, g_mlstm_norm[l].reshape(1, hv),
                      ln_w[l].reshape(1, cr), ln_b[l].reshape(1, cr), w_proj_mlstm[l].astype(BF16),
                      w_proj_rwkv[l].astype(BF16), w_out[l].astype(BF16), nt_out)

        i = l // 2
        if l % 2 == 0:
            assert need_ctx
            x_all = ffn_dense(x_mid, g_norm_ffn[l], modtab, w_ff_gate[i].astype(BF16), w_ff_up[i].astype(BF16),
                              w_ff_down[i].astype(BF16), w_ff_gate.shape[2] // 2)
        else:
            assert not need_ctx
            wr = jnp.concatenate([w_router[i], jnp.zeros((d, LANES - N_EXPERTS), F32)], axis=1)
            out = moe_final(x_mid, g_norm_ffn[l], modtab, wr, w_exp_gate[i].astype(BF16), w_exp_up[i].astype(BF16),
                            w_exp_down[i].astype(BF16), g_final, w_exp_gate.shape[3] // 2)
    return out
```

```python
import functools

import jax
import jax.numpy as jnp
import numpy as np
from jax import lax
from jax.experimental import pallas as pl
from jax.experimental.pallas import tpu as pltpu

F32 = jnp.float32
BF16 = jnp.bfloat16
HIGHEST = lax.Precision.HIGHEST

GRID_W = 64
MLSTM_HEADS = 8
MLSTM_DQK = 64
MLSTM_DV = 128
GATE_SOFT_CAP = 15.0
RWKV_HEADS = 16
RWKV_N = 64
DECAY_LORA = 64
AAA_LORA = 64
GATE_LORA = 128
GN_EPS = 64e-5
N_EXPERTS = 8
NORM_EPS = 1e-6

LANES = 128
MXU_DIM = 256
ROW_TILE = 256
PROJ_ROWS = 1024
VMEM_LIMIT = 56 * 1024 * 1024

SH1, SC1, GT1, SH2, SC2, GT2 = range(6)
MOD_ROWS = 8


def _cparams(sem):
    return pltpu.CompilerParams(dimension_semantics=sem, vmem_limit_bytes=VMEM_LIMIT)


def _sigmoid(x):
    return 1.0 / (1.0 + jnp.exp(-x))


def _block_diag_ones(block):
    r = lax.broadcasted_iota(jnp.int32, (MXU_DIM, MXU_DIM), 0)
    c = lax.broadcasted_iota(jnp.int32, (MXU_DIM, MXU_DIM), 1)
    sh = int(np.log2(block))
    return jnp.where((r >> sh) == (c >> sh), 1.0, 0.0).astype(BF16)


def _group_sum(x, bd):
    outs = []
    for c in range(x.shape[1] // MXU_DIM):
        xs = x[:, c * MXU_DIM:(c + 1) * MXU_DIM]
        hi = xs.astype(BF16)
        r1 = xs - hi.astype(F32)
        mid = r1.astype(BF16)
        lo = (r1 - mid.astype(F32)).astype(BF16)
        acc = jnp.dot(hi, bd, preferred_element_type=F32)
        acc += jnp.dot(mid, bd, preferred_element_type=F32)
        acc += jnp.dot(lo, bd, preferred_element_type=F32)
        outs.append(acc)
    return jnp.concatenate(outs, axis=1)


def _rms_mod(x, g, mod, shift_row, scale_row):
    ms = jnp.mean(x * x, axis=-1, keepdims=True)
    y = x * lax.rsqrt(ms + NORM_EPS) * g
    return y * (1.0 + mod[scale_row:scale_row + 1, :]) + mod[shift_row:shift_row + 1, :]


def _ada_kernel(s_ref, w_ref, b_ref, o_ref):
    s = s_ref[...]
    s = s * _sigmoid(s)
    o_ref[...] = jnp.dot(s, w_ref[...], precision=HIGHEST, preferred_element_type=F32) + b_ref[...]


def ada_mod(cond, w_ada, b_ada):
    depth, d, n = w_ada.shape
    tn = n // 4
    return pl.pallas_call(
        _ada_kernel,
        grid=(depth, n // tn),
        in_specs=[pl.BlockSpec((MOD_ROWS, d), lambda l, j: (0, 0)),
                  pl.BlockSpec((None, d, tn), lambda l, j: (l, 0, j)),
                  pl.BlockSpec((None, 1, tn), lambda l, j: (l, 0, j))],
        out_specs=pl.BlockSpec((None, MOD_ROWS, tn), lambda l, j: (l, 0, j)),
        out_shape=jax.ShapeDtypeStruct((depth, MOD_ROWS, n), F32),
        compiler_params=_cparams(("arbitrary", "arbitrary")),
        name="ada_mod",
    )(cond, w_ada, b_ada.reshape(depth, 1, n))


def _norm_mod_kernel(x_ref, g_ref, mod_ref, hb_ref, hf_ref):
    h = _rms_mod(x_ref[...], g_ref[...], mod_ref[...], SH1, SC1)
    hb_ref[...] = h.astype(BF16)
    hf_ref[...] = h


def norm_mod(x_all, g, modtab):
    b, tt, d = x_all.shape
    nt = tt // ROW_TILE
    ctx_tile = nt - 1
    spec = pl.BlockSpec((None, ROW_TILE, d), lambda bi, i: (bi, i, 0))
    return pl.pallas_call(
        _norm_mod_kernel,
        grid=(b, nt),
        in_specs=[spec,
                  pl.BlockSpec((1, d), lambda bi, i: (0, 0)),
                  pl.BlockSpec((None, None, MOD_ROWS, d), lambda bi, i: (bi, jnp.where(i == ctx_tile, 0, 1), 0, 0))],
        out_specs=[spec, spec],
        out_shape=[jax.ShapeDtypeStruct((b, tt, d), BF16), jax.ShapeDtypeStruct((b, tt, d), F32)],
        compiler_params=_cparams(("parallel", "parallel")),
        name="norm_mod",
    )(x_all, g.reshape(1, d), modtab)


def _mm_kernel(a_ref, w_ref, o_ref, *, precision):
    o_ref[...] = jnp.dot(a_ref[...], w_ref[...], precision=precision,
                         preferred_element_type=F32).astype(o_ref.dtype)


def matmul(a, w, tm, tn, out_dtype=F32, precision=None):
    m, k = a.shape
    n = w.shape[1]
    return pl.pallas_call(
        functools.partial(_mm_kernel, precision=precision),
        grid=(n // tn, m // tm),
        in_specs=[pl.BlockSpec((tm, k), lambda j, i: (i, 0)),
                  pl.BlockSpec((k, tn), lambda j, i: (0, j))],
        out_specs=pl.BlockSpec((tm, tn), lambda j, i: (i, j)),
        out_shape=jax.ShapeDtypeStruct((m, n), out_dtype),
        compiler_params=_cparams(("parallel", "parallel")),
        name="matmul",
    )(a, w)


def _mlstm_kernel(qf, kf, vf, gif, gff, qb, kb, vb, gib, gfb, bi_ref, bf_ref, of, ob, ct_ref, m_ref):
    @pl.when(pl.program_id(1) == 0)
    def _():
        ct_ref[...] = jnp.zeros_like(ct_ref)
        m_ref[...] = jnp.zeros_like(m_ref)

    gates = (_mlstm_gates(gif, gff, bi_ref, bf_ref, m_ref.at[0], False),
             _mlstm_gates(gib, gfb, bi_ref, bf_ref, m_ref.at[1], True))
    refs = ((qf, kf, vf, of), (qb, kb, vb, ob))
    for h in range(MLSTM_HEADS):
        for d in range(2):
            _mlstm_head(h, *refs[d], ct_ref.at[d], gates[d], bool(d))


def _running_max(x, rev):
    n = x.shape[0]
    rowi = lax.broadcasted_iota(jnp.int32, (n, 1), 0)
    sh = 1
    while sh < n:
        if rev:
            x = jnp.maximum(x, jnp.where(rowi < n - sh, pltpu.roll(x, n - sh, 0), -jnp.inf))
        else:
            x = jnp.maximum(x, jnp.where(rowi >= sh, pltpu.roll(x, sh, 0), -jnp.inf))
        sh *= 2
    return x


def _mlstm_gates(gi_ref, gf_ref, bi_ref, bf_ref, m_ref, rev):
    L = ROW_TILE

    def cap(a):
        return GATE_SOFT_CAP * jnp.tanh(a / GATE_SOFT_CAP)

    ic = cap(gi_ref[...] + bi_ref[...])
    fp = cap(gf_ref[...] + bf_ref[...])
    lf = jnp.minimum(fp, 0.0) - jnp.log(1.0 + jnp.exp(-jnp.abs(fp)))
    row = lax.broadcasted_iota(jnp.int32, (L, L), 0)
    col = lax.broadcasted_iota(jnp.int32, (L, L), 1)
    tri = (col >= row) if rev else (col <= row)
    b = jnp.dot(jnp.where(tri, 1.0, 0.0), lf, precision=HIGHEST, preferred_element_type=F32)
    src = ic - b
    m_prev = m_ref[0:1, :]
    mu = jnp.maximum(m_prev, _running_max(src, rev))
    w_inter = jnp.exp(m_prev - mu)
    e_den = jnp.exp(-(b + mu))
    last = 0 if rev else L - 1
    b_last = b[last:last + 1, :]
    w_src = b_last - b + ic
    m_new = jnp.maximum(b_last + m_prev, jnp.max(w_src, axis=0, keepdims=True))
    a_src = jnp.exp(w_src - m_new)
    a_old = jnp.exp(b_last + m_prev - m_new)
    m_ref[...] = jnp.broadcast_to(m_new, m_ref.shape)
    return src.T, mu, w_inter, e_den, a_src, a_old, tri


def _mlstm_head(h, q_ref, k_ref, v_ref, o_ref, ct_ref, gates, rev):
    L = ROW_TILE
    H, DQK, DV = MLSTM_HEADS, MLSTM_DQK, MLSTM_DV
    src_t, mu_all, w_inter_all, e_den_all, a_src_all, a_old_all, tri = gates
    lane0 = lax.broadcasted_iota(jnp.int32, (L, LANES), 1) == 0
    ones_blk = jnp.where(lane0, 1.0, 0.0).astype(BF16)
    scale = DQK ** -0.5
    ln = (H if rev else 0) + h
    col = lambda a: a[:, ln:ln + 1]
    w_intra = jnp.exp(jnp.where(tri, src_t[ln:ln + 1, :] - col(mu_all), -jnp.inf))
    w_inter = col(w_inter_all)
    qb = (q_ref[:, h * DQK:(h + 1) * DQK] * scale).astype(BF16)
    kf = k_ref[:, h * DQK:(h + 1) * DQK]
    kb = kf.astype(BF16)
    vaug = jnp.concatenate([v_ref[:, h * DV:(h + 1) * DV].astype(BF16), ones_blk], axis=1)
    sqk = lax.dot_general(qb, kb, (((1,), (1,)), ((), ())), preferred_element_type=F32)
    sw = (sqk * w_intra).astype(BF16)
    ct = ct_ref[h]
    qw = (q_ref[:, h * DQK:(h + 1) * DQK] * (scale * w_inter)).astype(BF16)
    num_aug = jnp.dot(jnp.concatenate([sw, qw], axis=1), jnp.concatenate([vaug, ct.astype(BF16)], axis=0),
                      preferred_element_type=F32)
    num = num_aug[:, :DV]
    den = num_aug[:, DV:DV + 1]
    o_ref[:, h * DV:(h + 1) * DV] = num / jnp.maximum(jnp.abs(den), col(e_den_all))
    ks = (kf * col(a_src_all)).astype(BF16)
    upd = lax.dot_general(ks, vaug, (((0,), (0,)), ((), ())), preferred_element_type=F32)
    ct_ref[h] = col(a_old_all) * ct + upd


def mlstm_scan(qkvo, gates, bias_i, bias_f):
    b, tt, _ = qkvo.shape
    nt = tt // ROW_TILE
    ctx_tile = nt - 1
    hq = MLSTM_HEADS * MLSTM_DQK
    hv = MLSTM_HEADS * MLSTM_DV
    order_f = lambda s: jnp.where(s == 0, ctx_tile, s - 1)
    order_b = lambda s: jnp.where(s == 0, ctx_tile, ctx_tile - s)

    def operands(order):
        return [pl.BlockSpec((None, ROW_TILE, hq), lambda bi, s: (bi, order(s), 0)),
                pl.BlockSpec((None, ROW_TILE, hq), lambda bi, s: (bi, order(s), 1)),
                pl.BlockSpec((None, ROW_TILE, hv), lambda bi, s: (bi, order(s), 1)),
                pl.BlockSpec((None, ROW_TILE, LANES), lambda bi, s: (bi, order(s), 0)),
                pl.BlockSpec((None, ROW_TILE, LANES), lambda bi, s: (bi, order(s), 1))]

    bias = pl.BlockSpec((1, LANES), lambda bi, s: (0, 0))
    return pl.pallas_call(
        _mlstm_kernel,
        grid=(b, nt),
        in_specs=operands(order_f) + operands(order_b) + [bias, bias],
        out_specs=[pl.BlockSpec((None, ROW_TILE, hv), lambda bi, s: (bi, order_f(s), 0)),
                   pl.BlockSpec((None, ROW_TILE, hv), lambda bi, s: (bi, order_b(s), 0))],
        out_shape=[jax.ShapeDtypeStruct((b, tt, hv), F32)] * 2,
        scratch_shapes=[pltpu.VMEM((2, MLSTM_HEADS, MLSTM_DQK, 2 * MLSTM_DV), F32),
                        pltpu.VMEM((2, 8, LANES), F32)],
        compiler_params=_cparams(("parallel", "arbitrary")),
        name="mlstm",
    )(*([qkvo] * 3 + [gates] * 2) * 2, bias_i, bias_f)


OP_KK, OP_R, OP_WF, OP_BF, OP_KF, OP_WB, OP_BB, OP_KB = range(8)
N_OPS = 8


def _rwkv_prep_kernel(p_ref, hp_ref, hn_ref, mu_ref, w0_ref, w2_ref, a0_ref, a2_ref, g2_ref, kk_ref, ka_ref,
                      rk_ref, ops_o, v_o, g_o, bonus_o, xs_ref):
    T = ROW_TILE
    i = pl.program_id(1)
    n_lat = pl.num_programs(1) - 1
    x = p_ref[...]
    W = x.shape[1]
    rowi = lax.broadcasted_iota(jnp.int32, (T, 1), 0)
    coli = lax.broadcasted_iota(jnp.int32, (1, W), 1)
    prev = pltpu.roll(x, 1, 0)
    nxt = pltpu.roll(x, T - 1, 0)

    @pl.when(i < n_lat)
    def _():
        gc = rowi & (GRID_W - 1)
        left = jnp.where(gc == 0, 0.0, prev)
        right = jnp.where(gc == GRID_W - 1, 0.0, nxt)
        hp = jnp.where(i == 0, 0.0, hp_ref[...])
        hn = jnp.where(i == n_lat - 1, 0.0, hn_ref[...])
        up = jnp.concatenate([hp, x[:T - GRID_W]], axis=0)
        down = jnp.concatenate([x[GRID_W:], hn], axis=0)
        q = W // 4
        xs_ref[...] = jnp.where(coli < q, left, jnp.where(coli < 2 * q, right, jnp.where(coli < 3 * q, up, down)))

    @pl.when(i == n_lat)
    def _():
        xs_ref[...] = jnp.where(coli < W // 2, jnp.where(rowi == 0, 0.0, prev), jnp.where(rowi == T - 1, 0.0, nxt))

    xm = x + (xs_ref[...] - x) * mu_ref[...]
    C = RWKV_HEADS * RWKV_N
    r = xm[:, 0:C]
    k = xm[:, C:2 * C]
    v = xm[:, 2 * C:3 * C]
    wd = jnp.tanh(xm[:, 3 * C:3 * C + 2 * DECAY_LORA]).astype(BF16)
    ad = xm[:, 3 * C + 2 * DECAY_LORA:3 * C + 2 * DECAY_LORA + 2 * AAA_LORA].astype(BF16)
    gd = _sigmoid(xm[:, 3 * C + 2 * DECAY_LORA + 2 * AAA_LORA:]).astype(BF16)
    bd = _block_diag_ones(RWKV_N)

    kk = k * kk_ref[...]
    nrm = jnp.sqrt(_group_sum(kk * kk, bd))
    kk = kk / jnp.maximum(nrm, 1e-12)
    u = w0_ref[...] + jnp.dot(wd, w2_ref[...], preferred_element_type=F32)
    decay = jnp.exp(-np.float32(np.exp(-0.5)) * _sigmoid(u))
    a = _sigmoid(a0_ref[...] + jnp.dot(ad, a2_ref[...], preferred_element_type=F32))
    ka = ka_ref[...]
    kd_f = k * (1.0 + (a[:, :C] - 1.0) * ka)
    kd_b = k * (1.0 + (a[:, C:] - 1.0) * ka)
    ops_o[OP_R] = r.T
    v_o[...] = v.T
    ops_o[OP_KK] = kk.T
    ops_o[OP_WF] = decay[:, :C].T
    ops_o[OP_WB] = decay[:, C:].T
    ops_o[OP_BF] = (kk * a[:, :C]).T
    ops_o[OP_BB] = (kk * a[:, C:]).T
    ops_o[OP_KF] = kd_f.T
    ops_o[OP_KB] = kd_b.T
    g_o[...] = jnp.dot(gd, g2_ref[...], preferred_element_type=F32)
    bonus_o[...] = _group_sum(r * (kd_f + kd_b) * rk_ref[...], bd) * v


def rwkv_prepare(p, mu, w0cat, w2cat, a0cat, a2cat, g2, k_k, k_a, r_k):
    b, tt, w = p.shape
    nt = tt // ROW_TILE
    n_lat = nt - 1
    hpt = ROW_TILE // GRID_W
    c = RWKV_HEADS * RWKV_N
    n_halo = tt // GRID_W
    const = lambda shape: pl.BlockSpec(shape, lambda bi, i: (0,) * len(shape))
    out_spec = pl.BlockSpec((None, ROW_TILE, c), lambda bi, i: (bi, i, 0))
    return pl.pallas_call(
        _rwkv_prep_kernel,
        grid=(b, nt),
        in_specs=[pl.BlockSpec((None, ROW_TILE, w), lambda bi, i: (bi, i, 0)),
                  pl.BlockSpec((None, GRID_W, w), lambda bi, i: (bi, jnp.maximum(i * hpt - 1, 0), 0)),
                  pl.BlockSpec((None, GRID_W, w), lambda bi, i: (bi, jnp.minimum(i * hpt + hpt, n_halo - 1), 0)),
                  const((1, w)), const((1, 2 * c)), const((2 * DECAY_LORA, 2 * c)), const((1, 2 * c)),
                  const((2 * AAA_LORA, 2 * c)), const((GATE_LORA, c)), const((1, c)), const((1, c)), const((1, c))],
        out_specs=[pl.BlockSpec((N_OPS, None, c, ROW_TILE), lambda bi, i: (0, bi, 0, i)),
                   pl.BlockSpec((None, c, ROW_TILE), lambda bi, i: (bi, 0, i)), out_spec, out_spec],
        out_shape=[jax.ShapeDtypeStruct((N_OPS, b, c, tt), F32), jax.ShapeDtypeStruct((b, c, tt), F32),
                   jax.ShapeDtypeStruct((b, tt, c), F32), jax.ShapeDtypeStruct((b, tt, c), F32)],
        scratch_shapes=[pltpu.VMEM((ROW_TILE, w), F32)],
        compiler_params=_cparams(("parallel", "parallel")),
        name="rwkv_prepare",
    )(p, p, p, mu, w0cat, w2cat, a0cat, a2cat, g2, k_k, k_a, r_k)


RELAYOUT_ROWS = 128
V_HALF = RWKV_N // 2


def _to_scan_k_kernel(x_ref, o_ref):
    nbh = x_ref.shape[0] // RWKV_N
    for k in range(RWKV_N):
        g = x_ref[pl.ds(k, nbh, stride=RWKV_N), :]
        o_ref[k] = jnp.concatenate([g, g], axis=0).T


def to_scan_k(ops_t):
    n_ops, b, c, tt = ops_t.shape
    return pl.pallas_call(
        _to_scan_k_kernel,
        grid=(n_ops, tt // RELAYOUT_ROWS),
        in_specs=[pl.BlockSpec((None, b * c, RELAYOUT_ROWS), lambda o, i: (o, 0, i))],
        out_specs=pl.BlockSpec((None, RWKV_N, RELAYOUT_ROWS, LANES), lambda o, i: (o, 0, i, 0)),
        out_shape=jax.ShapeDtypeStruct((n_ops, RWKV_N, tt, LANES), F32),
        compiler_params=_cparams(("parallel", "parallel")),
        name="to_scan_k",
    )(ops_t.reshape(n_ops, b * c, tt))


def _to_scan_v_kernel(x_ref, o_ref):
    nbh = x_ref.shape[0] // RWKV_N
    for v in range(V_HALF):
        m = jnp.concatenate([x_ref[pl.ds(v, nbh, stride=RWKV_N), :],
                             x_ref[pl.ds(V_HALF + v, nbh, stride=RWKV_N), :]], axis=0)
        o_ref[:, v, :] = m.T


def to_scan_v(x_t):
    b, c, tt = x_t.shape
    return pl.pallas_call(
        _to_scan_v_kernel,
        grid=(tt // RELAYOUT_ROWS,),
        in_specs=[pl.BlockSpec((b * c, RELAYOUT_ROWS), lambda i: (0, i))],
        out_specs=pl.BlockSpec((RELAYOUT_ROWS, V_HALF, LANES), lambda i: (i, 0, 0)),
        out_shape=jax.ShapeDtypeStruct((tt, V_HALF, LANES), F32),
        compiler_params=_cparams(("parallel",)),
        name="to_scan_v",
    )(x_t.reshape(b * c, tt))


def _from_scan_kernel(yf_ref, yb_ref, o_ref, zs_ref):
    nbh = zs_ref.shape[0] // RWKV_N
    for v in range(V_HALF):
        t = (yf_ref[:, v, :] + yb_ref[:, v, :]).T
        zs_ref[pl.ds(v, nbh, stride=RWKV_N), :] = t[:nbh]
        zs_ref[pl.ds(V_HALF + v, nbh, stride=RWKV_N), :] = t[nbh:]
    for b in range(o_ref.shape[0]):
        for cb in range(o_ref.shape[2] // LANES):
            r0 = (b * (o_ref.shape[2] // LANES) + cb) * LANES
            o_ref[b, :, cb * LANES:(cb + 1) * LANES] = zs_ref[r0:r0 + LANES, :].T


def from_scan(yf, yb, b):
    tt = yf.shape[0]
    c = RWKV_HEADS * RWKV_N
    spec = pl.BlockSpec((RELAYOUT_ROWS, V_HALF, LANES), lambda i: (i, 0, 0))
    return pl.pallas_call(
        _from_scan_kernel,
        grid=(tt // RELAYOUT_ROWS,),
        in_specs=[spec, spec],
        out_specs=pl.BlockSpec((b, RELAYOUT_ROWS, c), lambda i: (0, i, 0)),
        out_shape=jax.ShapeDtypeStruct((b, tt, c), F32),
        scratch_shapes=[pltpu.VMEM((b * c, RELAYOUT_ROWS), F32)],
        compiler_params=_cparams(("parallel",)),
        name="from_scan",
    )(yf, yb)


SCAN_ROWS = 32


def _rwkv_scan_kernel(kk_f, kkn_f, r_f, w_f, b_f, k_f, kk_b, kkn_b, r_b, w_b, b_b, k_b, v_f, v_b, y_f, y_b,
                      s_ref, sa_ref, kkx_f, kkx_b):
    N, TB, PAD = RWKV_N, SCAN_ROWS, 8

    @pl.when(pl.program_id(0) == 0)
    def _():
        s_ref[...] = jnp.zeros_like(s_ref)
        sa_ref[...] = jnp.zeros_like(sa_ref)

    kkx_f[:, 0:TB, :] = kk_f[...]
    kkx_f[:, TB:TB + PAD, :] = kkn_f[:, 0:PAD, :]
    kkx_b[:, PAD:PAD + TB, :] = kk_b[...]
    kkx_b[:, 0:PAD, :] = kkn_b[:, TB - PAD:TB, :]

    dirs = ((kkx_f, r_f, w_f, b_f, k_f, v_f, y_f), (kkx_b, r_b, w_b, b_b, k_b, v_b, y_b))

    def step(j, carry):
        for d, (kkx_r, r_r, w_r, b_r, k_r, v_r, y_r) in enumerate(dirs):
            row, nxt = (j, j + 1) if d == 0 else (TB - 1 - j, PAD + TB - 2 - j)
            op = lambda ref, kx, rw: ref[kx, pl.ds(rw, 1), :]
            sa = sa_ref[d]
            vt = v_r[row]
            acc_sa = [jnp.zeros((V_HALF, LANES), F32), jnp.zeros((V_HALF, LANES), F32)]
            acc_y = [jnp.zeros((V_HALF, LANES), F32), jnp.zeros((V_HALF, LANES), F32)]
            for kx in range(N):
                sk = s_ref[d, kx] * op(w_r, kx, row) - sa * op(b_r, kx, row) + vt * op(k_r, kx, row)
                s_ref[d, kx] = sk
                acc_sa[kx % 2] = acc_sa[kx % 2] + sk * op(kkx_r, kx, nxt)
                acc_y[kx % 2] = acc_y[kx % 2] + sk * op(r_r, kx, row)
            sa_ref[d] = acc_sa[0] + acc_sa[1]
            y_r[row] = acc_y[0] + acc_y[1]
        return carry

    lax.fori_loop(0, SCAN_ROWS, step, 0)


def rwkv_scan(ops_s, v_s, seq):
    _, n, tt, lanes = ops_s.shape
    n_lat = seq // SCAN_ROWS
    n_blk = tt // SCAN_ROWS
    n_ctx = n_blk - n_lat
    blk_f = lambda s: jnp.where(s < n_ctx, n_lat + s, s - n_ctx)
    blk_b = lambda s: n_blk - 1 - s
    kspec = lambda o, blk: pl.BlockSpec((None, n, SCAN_ROWS, lanes), lambda s: (o, 0, blk(s), 0))
    vspec = lambda blk: pl.BlockSpec((SCAN_ROWS, V_HALF, lanes), lambda s: (blk(s), 0, 0))
    nxt = lambda blk: (lambda s: blk(jnp.minimum(s + 1, n_blk - 1)))
    fwd = [kspec(OP_KK, blk_f), kspec(OP_KK, nxt(blk_f))] + [kspec(o, blk_f) for o in (OP_R, OP_WF, OP_BF, OP_KF)]
    bwd = [kspec(OP_KK, blk_b), kspec(OP_KK, nxt(blk_b))] + [kspec(o, blk_b) for o in (OP_R, OP_WB, OP_BB, OP_KB)]
    return pl.pallas_call(
        _rwkv_scan_kernel,
        grid=(n_blk,),
        in_specs=fwd + bwd + [vspec(blk_f), vspec(blk_b)],
        out_specs=[vspec(blk_f), vspec(blk_b)],
        out_shape=[jax.ShapeDtypeStruct((tt, V_HALF, lanes), F32)] * 2,
        scratch_shapes=[pltpu.VMEM((2, n, V_HALF, lanes), F32), pltpu.VMEM((2, V_HALF, lanes), F32),
                        pltpu.VMEM((n, SCAN_ROWS + 8, lanes), F32), pltpu.VMEM((n, SCAN_ROWS + 8, lanes), F32)],
        compiler_params=_cparams(("arbitrary",)),
        name="rwkv_scan",
    )(*([ops_s] * 12), v_s, v_s)


def _merge_kernel(hf_ref, hb_ref, o_ref, y_ref, bonus_ref, g_ref, mg_ref, x_ref, mod_ref, gm_ref, lnw_ref,
                  lnb_ref, wpm_ref, wpr_ref, wout_ref, out_ref):
    c = RWKV_HEADS * RWKV_N
    hm = hf_ref[...] + hb_ref[...]
    ms = _group_sum(hm * hm, _block_diag_ones(MLSTM_DV)) * (1.0 / MLSTM_DV)
    hmn = hm * lax.rsqrt(ms + NORM_EPS) * gm_ref[...] * _sigmoid(o_ref[...])
    ym = jnp.dot(hmn.astype(BF16), wpm_ref[...], preferred_element_type=F32)
    bd = _block_diag_ones(RWKV_N)
    y = y_ref[...]
    yc = y - _group_sum(y, bd) * (1.0 / RWKV_N)
    var = _group_sum(yc * yc, bd) * (1.0 / RWKV_N)
    yn = yc * lax.rsqrt(var + GN_EPS) * lnw_ref[...] + lnb_ref[...]
    yr_in = ((yn + bonus_ref[...]) * g_ref[...]).astype(BF16)
    yr = jnp.dot(yr_in, wpr_ref[...], preferred_element_type=F32)
    mg = mg_ref[...]
    z = _sigmoid(mg[:, :c]) * ym + _sigmoid(mg[:, c:]) * yr
    yy = jnp.dot(z.astype(BF16), wout_ref[...], preferred_element_type=F32)
    out_ref[...] = x_ref[...] + mod_ref[GT1:GT1 + 1, :] * yy


def merge(hf, hb, qkvo, y, bonus, g, mg, x_all, modtab, gm, lnw, lnb, wpm, wpr, wout, nt_out):
    b, tt, d = x_all.shape
    ctx_tile = tt // ROW_TILE - 1
    row = lambda width, colblk=0: pl.BlockSpec((None, ROW_TILE, width), lambda bi, i: (bi, i, colblk))
    const = lambda shape: pl.BlockSpec(shape, lambda bi, i: (0,) * len(shape))
    return pl.pallas_call(
        _merge_kernel,
        grid=(b, nt_out),
        in_specs=[row(d), row(d), row(d, 2), row(d), row(d), row(d), row(2 * d), row(d),
                  pl.BlockSpec((None, None, MOD_ROWS, d), lambda bi, i: (bi, jnp.where(i == ctx_tile, 0, 1), 0, 0)),
                  const((1, d)), const((1, d)), const((1, d)), const((d, d)), const((d, d)), const((d, d))],
        out_specs=row(d),
        out_shape=jax.ShapeDtypeStruct((b, nt_out * ROW_TILE, d), F32),
        compiler_params=_cparams(("parallel", "parallel")),
        name="merge",
    )(hf, hb, qkvo, y, bonus, g, mg, x_all, modtab, gm, lnw, lnb, wpm, wpr, wout)


def _ffn_kernel(x_ref, g_ref, mod_ref, wg_ref, wu_ref, wd_ref, o_ref, h_scr, acc_scr):
    j = pl.program_id(2)

    @pl.when(j == 0)
    def _():
        h_scr[...] = _rms_mod(x_ref[...], g_ref[...], mod_ref[...], SH2, SC2).astype(BF16)
        acc_scr[...] = jnp.zeros_like(acc_scr)

    h = h_scr[...]
    a = jnp.dot(h, wg_ref[...], preferred_element_type=F32)
    u = jnp.dot(h, wu_ref[...], preferred_element_type=F32)
    t = (a * _sigmoid(a) * u).astype(BF16)
    acc_scr[...] += jnp.dot(t, wd_ref[...], preferred_element_type=F32)

    @pl.when(j == pl.num_programs(2) - 1)
    def _():
        o_ref[...] = x_ref[...] + mod_ref[GT2:GT2 + 1, :] * acc_scr[...]


def ffn_dense(x_all, g, modtab, wg, wu, wd, tf):
    b, tt, d = x_all.shape
    nt = tt // ROW_TILE
    ctx_tile = nt - 1
    f = wg.shape[1]
    row = pl.BlockSpec((None, ROW_TILE, d), lambda bi, i, j: (bi, i, 0))
    return pl.pallas_call(
        _ffn_kernel,
        grid=(b, nt, f // tf),
        in_specs=[row,
                  pl.BlockSpec((1, d), lambda bi, i, j: (0, 0)),
                  pl.BlockSpec((None, None, MOD_ROWS, d),
                               lambda bi, i, j: (bi, jnp.where(i == ctx_tile, 0, 1), 0, 0)),
                  pl.BlockSpec((d, tf), lambda bi, i, j: (0, j)),
                  pl.BlockSpec((d, tf), lambda bi, i, j: (0, j)),
                  pl.BlockSpec((tf, d), lambda bi, i, j: (j, 0))],
        out_specs=row,
        out_shape=jax.ShapeDtypeStruct((b, tt, d), F32),
        scratch_shapes=[pltpu.VMEM((ROW_TILE, d), BF16), pltpu.VMEM((ROW_TILE, d), F32)],
        compiler_params=_cparams(("parallel", "parallel", "arbitrary")),
        name="ffn_dense",
    )(x_all, g.reshape(1, d), modtab, wg, wu, wd)


MOE_ROWS = 512


TOP_K = 2
MOE_TOKENS = 256
SEL_E1, SEL_E2, SEL_P1, SEL_P2 = range(4)


def _moe_route_kernel(x_ref, g_ref, mod_ref, wr_ref, h_o, sel_o):
    h = _rms_mod(x_ref[...], g_ref[...], mod_ref[...], SH2, SC2)
    h_o[...] = h
    logits = jnp.dot(h, wr_ref[...], precision=HIGHEST, preferred_element_type=F32)
    lane = lax.broadcasted_iota(jnp.int32, logits.shape, 1)
    logits = jnp.where(lane < N_EXPERTS, logits, -jnp.inf)
    m1 = jnp.max(logits, axis=-1, keepdims=True)
    i1 = jnp.min(jnp.where(logits == m1, lane, LANES), axis=-1, keepdims=True)
    rest = jnp.where(lane == i1, -jnp.inf, logits)
    m2 = jnp.max(rest, axis=-1, keepdims=True)
    i2 = jnp.min(jnp.where(rest == m2, lane, LANES), axis=-1, keepdims=True)
    e21 = jnp.exp(m2 - m1)
    p1 = 1.0 / (1.0 + e21)
    sel_o[...] = jnp.where(lane == SEL_E1, i1.astype(F32),
                           jnp.where(lane == SEL_E2, i2.astype(F32),
                                     jnp.where(lane == SEL_P1, p1, jnp.where(lane == SEL_P2, e21 * p1, 0.0))))


def moe_route(x_lat, g, modtab, w_router):
    b, s, d = x_lat.shape
    nt = s // MOE_ROWS
    return pl.pallas_call(
        _moe_route_kernel,
        grid=(b, nt),
        in_specs=[pl.BlockSpec((None, MOE_ROWS, d), lambda bi, i: (bi, i, 0)),
                  pl.BlockSpec((1, d), lambda bi, i: (0, 0)),
                  pl.BlockSpec((None, None, MOD_ROWS, d), lambda bi, i: (bi, 1, 0, 0)),
                  pl.BlockSpec((d, LANES), lambda bi, i: (0, 0))],
        out_specs=[pl.BlockSpec((MOE_ROWS, d), lambda bi, i: (bi * nt + i, 0)),
                   pl.BlockSpec((MOE_ROWS, LANES), lambda bi, i: (bi * nt + i, 0))],
        out_shape=[jax.ShapeDtypeStruct((b * s, d), F32), jax.ShapeDtypeStruct((b * s, LANES), F32)],
        compiler_params=_cparams(("parallel", "parallel")),
        name="moe_route",
    )(x_lat, g.reshape(1, d), modtab, w_router)


def _row_copy(src, src_row, dst, dst_row, sem):
    return pltpu.make_async_copy(src.at[pl.ds(src_row, 1)], dst.at[pl.ds(dst_row, 1)], sem)


ORDER_CHUNK = 1024


def _moe_dispatch_kernel(s0_ref, ord_lo, ord_hi, h_hbm, xg_ref, sem, *, n_choices):
    s0 = s0_ref[pl.program_id(0)]
    base = s0 - (s0 & (ORDER_CHUNK - 1))

    def start(r, c):
        j = jnp.minimum(s0 + r, n_choices - 1) - base
        choice = jnp.where(j < ORDER_CHUNK, ord_lo[jnp.minimum(j, ORDER_CHUNK - 1)],
                           ord_hi[jnp.maximum(j - ORDER_CHUNK, 0)])
        _row_copy(h_hbm, lax.shift_right_logical(choice, 1), xg_ref, r, sem).start()
        return c

    def wait(r, c):
        _row_copy(h_hbm, 0, xg_ref, 0, sem).wait()
        return c

    lax.fori_loop(0, MOE_TOKENS, start, 0, unroll=8)
    lax.fori_loop(0, MOE_TOKENS, wait, 0, unroll=8)


def moe_dispatch(s0, order, h, rows):
    d = h.shape[1]
    n_chunks = order.shape[0] // ORDER_CHUNK
    return pl.pallas_call(
        functools.partial(_moe_dispatch_kernel, n_choices=order.shape[0]),
        grid_spec=pltpu.PrefetchScalarGridSpec(
            num_scalar_prefetch=1,
            grid=(rows // MOE_TOKENS,),
            in_specs=[pl.BlockSpec((ORDER_CHUNK,), lambda i, s: (s[i] // ORDER_CHUNK,), memory_space=pltpu.SMEM),
                      pl.BlockSpec((ORDER_CHUNK,), lambda i, s: (jnp.minimum(s[i] // ORDER_CHUNK + 1, n_chunks - 1),),
                                   memory_space=pltpu.SMEM),
                      pl.BlockSpec(memory_space=pl.ANY)],
            out_specs=pl.BlockSpec((MOE_TOKENS, d), lambda i, s: (i, 0)),
            scratch_shapes=[pltpu.SemaphoreType.DMA(())]),
        out_shape=jax.ShapeDtypeStruct((rows, d), F32),
        compiler_params=_cparams(("arbitrary",)),
        name="moe_dispatch",
    )(s0, order, order, h)


def _moe_experts_kernel(te_ref, nu_ref, xg_ref, wg_ref, wu_ref, wd_ref, y_ref, h_scr, acc_scr):
    del te_ref
    i = pl.program_id(0)
    j = pl.program_id(1)
    used = i < nu_ref[0]
    last = j == pl.num_programs(1) - 1

    @pl.when(jnp.logical_and(used, j == 0))
    def _():
        h_scr[...] = xg_ref[...].astype(BF16)
        acc_scr[...] = jnp.zeros_like(acc_scr)

    @pl.when(used)
    def _():
        h = h_scr[...]
        a = jnp.dot(h, wg_ref[...], preferred_element_type=F32)
        u = jnp.dot(h, wu_ref[...], preferred_element_type=F32)
        t = (a * _sigmoid(a) * u).astype(BF16)
        acc_scr[...] += jnp.dot(t, wd_ref[...], preferred_element_type=F32)

    @pl.when(jnp.logical_and(used, last))
    def _():
        y_ref[...] = acc_scr[...]

    @pl.when(jnp.logical_and(jnp.logical_not(used), last))
    def _():
        y_ref[...] = jnp.zeros_like(y_ref)


def moe_experts(tile_expert, n_used, xg, wg, wu, wd, tf):
    rows, d = xg.shape
    f = wg.shape[2]
    return pl.pallas_call(
        _moe_experts_kernel,
        grid_spec=pltpu.PrefetchScalarGridSpec(
            num_scalar_prefetch=2,
            grid=(rows // MOE_ROWS, f // tf),
            in_specs=[pl.BlockSpec((MOE_ROWS, d), lambda i, j, te, nu: (i, 0)),
                      pl.BlockSpec((None, d, tf), lambda i, j, te, nu: (te[i], 0, j)),
                      pl.BlockSpec((None, d, tf), lambda i, j, te, nu: (te[i], 0, j)),
                      pl.BlockSpec((None, tf, d), lambda i, j, te, nu: (te[i], j, 0))],
            out_specs=pl.BlockSpec((MOE_ROWS, d), lambda i, j, te, nu: (i, 0)),
            scratch_shapes=[pltpu.VMEM((MOE_ROWS, d), BF16), pltpu.VMEM((MOE_ROWS, d), F32)]),
        out_shape=jax.ShapeDtypeStruct((rows, d), F32),
        compiler_params=_cparams(("arbitrary", "arbitrary")),
        name="moe_experts",
    )(tile_expert, n_used, xg, wg, wu, wd)


def _moe_combine_kernel(pos_ref, sel_ref, x_ref, mod_ref, gfin_ref, y_hbm, o_ref, ybuf, sem):
    n = pos_ref.shape[-1]

    def start(a, c):
        _row_copy(y_hbm, pos_ref[0, a], ybuf.at[a & 1], lax.shift_right_logical(a, 1), sem).start()
        return c

    def wait(a, c):
        _row_copy(y_hbm, 0, ybuf.at[0], 0, sem).wait()
        return c

    lax.fori_loop(0, n, start, 0, unroll=8)
    lax.fori_loop(0, n, wait, 0, unroll=8)
    sel = sel_ref[...]
    yy = sel[:, SEL_P1:SEL_P1 + 1] * ybuf[0] + sel[:, SEL_P2:SEL_P2 + 1] * ybuf[1]
    xo = x_ref[...] + mod_ref[GT2:GT2 + 1, :] * yy
    ms = jnp.mean(xo * xo, axis=-1, keepdims=True)
    o_ref[...] = xo * lax.rsqrt(ms + NORM_EPS) * gfin_ref[...]


def moe_combine(pos, sel, x_lat, modtab, g_final, y):
    b, s, d = x_lat.shape
    nt = s // MOE_TOKENS
    n = TOP_K * MOE_TOKENS
    row = pl.BlockSpec((None, MOE_TOKENS, d), lambda bi, i: (bi, i, 0))
    return pl.pallas_call(
        _moe_combine_kernel,
        grid=(b, nt),
        in_specs=[pl.BlockSpec((None, 1, n), lambda bi, i: (bi * nt + i, 0, 0), memory_space=pltpu.SMEM),
                  pl.BlockSpec((MOE_TOKENS, LANES), lambda bi, i: (bi * nt + i, 0)),
                  row,
                  pl.BlockSpec((None, None, MOD_ROWS, d), lambda bi, i: (bi, 1, 0, 0)),
                  pl.BlockSpec((1, d), lambda bi, i: (0, 0)),
                  pl.BlockSpec(memory_space=pl.ANY)],
        out_specs=row,
        out_shape=jax.ShapeDtypeStruct((b, s, d), F32),
        scratch_shapes=[pltpu.VMEM((TOP_K, MOE_TOKENS, d), F32), pltpu.SemaphoreType.DMA(())],
        compiler_params=_cparams(("arbitrary", "arbitrary")),
        name="moe_combine",
    )(pos.reshape(b * nt, 1, n), sel, x_lat, modtab, g_final.reshape(1, d), y)


def moe_final(x_lat, g, modtab, w_router, wg, wu, wd, g_final, tf):
    b, s, d = x_lat.shape
    m = b * s
    ne = wg.shape[0]
    h, sel = moe_route(x_lat, g, modtab, w_router)
    e = sel[:, SEL_E1:SEL_E2 + 1].astype(jnp.int32).reshape(m * TOP_K)
    onehot = (e[:, None] == jnp.arange(ne, dtype=jnp.int32)[None, :]).astype(jnp.int32)
    running = jnp.cumsum(onehot, axis=0)
    counts = running[-1]
    padded = (counts + MOE_ROWS - 1) // MOE_ROWS * MOE_ROWS
    group_end = jnp.cumsum(padded)
    group_start = group_end - padded
    pos = jnp.sum(onehot * (group_start[None, :] + running - 1), axis=1).astype(jnp.int32)
    n_tiles = m * TOP_K // MOE_ROWS + ne
    tile_expert = jnp.minimum(
        jnp.sum(jnp.arange(n_tiles, dtype=jnp.int32)[:, None] >= (group_end // MOE_ROWS)[None, :], axis=1),
        ne - 1).astype(jnp.int32)
    n_used = (group_end[-1:] // MOE_ROWS).astype(jnp.int32)
    order = jnp.argsort(e, stable=True).astype(jnp.int32)
    pad_before = group_start - (jnp.cumsum(counts) - counts)
    first_row = jnp.arange(n_tiles * MOE_ROWS // MOE_TOKENS, dtype=jnp.int32) * MOE_TOKENS
    s0 = jnp.clip(first_row - pad_before[tile_expert[first_row // MOE_ROWS]], 0,
                  m * TOP_K - 1).astype(jnp.int32)
    xg = moe_dispatch(s0, order, h, n_tiles * MOE_ROWS)
    y = moe_experts(tile_expert, n_used, xg, wg, wu, wd, tf)
    return moe_combine(pos, sel, x_lat, modtab, g_final, y)


def kernel(x, c, ctx, c_ctx, w_ada, b_ada, g_norm_mix, g_norm_ffn, w_in, b_mlstm_gate, g_mlstm_norm, mu_shift, w0, w2, a0, a2, g2, k_k, k_a, r_k, ln_w, ln_b, w_proj_mlstm, w_proj_rwkv, w_out, w_ff_gate, w_ff_up, w_ff_down, w_router, w_exp_gate, w_exp_up, w_exp_down, g_final):
    b, seq, d = x.shape
    ctx_len = ctx.shape[1]
    depth = w_ada.shape[0]
    assert ctx_len == ROW_TILE and seq % ROW_TILE == 0 and seq % MOE_ROWS == 0 and 2 * b * RWKV_HEADS == LANES
    assert (b * (seq + ctx_len)) % PROJ_ROWS == 0
    tt = seq + ctx_len
    m_all = b * tt
    H = MLSTM_HEADS
    cr = RWKV_HEADS * RWKV_N
    hq, hv = H * MLSTM_DQK, H * MLSTM_DV

    x_all = jnp.concatenate([x, ctx], axis=1)
    cond = jnp.concatenate([c, c_ctx[None, :], jnp.zeros((MOD_ROWS - b - 1, d), F32)], axis=0)
    mod_all = ada_mod(cond, w_ada, b_ada).reshape(depth, MOD_ROWS, 6, d)

    o_gate = 2 * hq + 2 * hv
    o_rwkv = o_gate + 4 * H
    rwkv_w = 3 * cr + 2 * DECAY_LORA + 2 * AAA_LORA + GATE_LORA
    o_mg = o_rwkv + rwkv_w

    out = None
    for l in range(depth):
        need_ctx = l < depth - 1
        pad = jnp.zeros((b, MOD_ROWS - 6, d), F32)
        mod_lat = jnp.concatenate([mod_all[l, :b], pad], axis=1)
        mod_ctx = jnp.broadcast_to(jnp.concatenate([mod_all[l, b], pad[0]], axis=0), (b, MOD_ROWS, d))
        modtab = jnp.stack([mod_ctx, mod_lat], axis=1)

        wl = w_in[l]
        w_qkvo = wl[:, :o_gate].astype(BF16)
        wgt = wl[:, o_gate:o_rwkv].reshape(d, 4, H)
        zpad = jnp.zeros((d, LANES - 2 * H), F32)
        w_gates = jnp.concatenate([wgt[:, 0], wgt[:, 2], zpad, wgt[:, 1], wgt[:, 3], zpad], axis=1)
        bg = b_mlstm_gate[l]
        lpad = jnp.zeros((LANES - 2 * H,), F32)
        bias_i = jnp.concatenate([bg[0], bg[2], lpad]).reshape(1, LANES)
        bias_f = jnp.concatenate([bg[1], bg[3], lpad]).reshape(1, LANES)
        w_rwkv = wl[:, o_rwkv:o_mg].astype(BF16)
        w_mg = wl[:, o_mg:].astype(BF16)

        hb16, hf32 = norm_mod(x_all, g_norm_mix[l], modtab)
        hb16 = hb16.reshape(m_all, d)
        qkvo = matmul(hb16, w_qkvo, PROJ_ROWS, 1024).reshape(b, tt, o_gate)
        gates = matmul(hf32.reshape(m_all, d), w_gates, 512, 2 * LANES, precision=HIGHEST).reshape(b, tt, 2 * LANES)
        prw = matmul(hb16, w_rwkv, PROJ_ROWS, rwkv_w // 3).reshape(b, tt, rwkv_w)
        mg = matmul(hb16, w_mg, PROJ_ROWS, 1024).reshape(b, tt, 2 * d)

        h_f, h_b = mlstm_scan(qkvo, gates, bias_i, bias_f)

        zl = jnp.zeros((DECAY_LORA, cr), F32)
        w2cat = jnp.concatenate([jnp.concatenate([w2[l, 0], zl], axis=1),
                                 jnp.concatenate([zl, w2[l, 1]], axis=1)], axis=0).astype(BF16)
        a2cat = jnp.concatenate([jnp.concatenate([a2[l, 0], zl], axis=1),
                                 jnp.concatenate([zl, a2[l, 1]], axis=1)], axis=0).astype(BF16)
        ops_, v_, g_, bonus_ = rwkv_prepare(
            prw, mu_shift[l].reshape(1, rwkv_w), w0[l].reshape(1, 2 * cr), w2cat, a0[l].reshape(1, 2 * cr), a2cat,
            g2[l].astype(BF16), k_k[l].reshape(1, cr), k_a[l].reshape(1, cr), r_k[l].reshape(1, cr))
        y_f, y_b = rwkv_scan(to_scan_k(ops_), to_scan_v(v_), seq)
        y_ = from_scan(y_f, y_b, b)

        nt_out = tt // ROW_TILE if need_ctx else seq // ROW_TILE
        x_mid = merge(h_f, h_b, qkvo, y_, bonus_, g_, mg, x_all, modtab, g_mlstm_norm[l].reshape(1, hv),
                      ln_w[l].reshape(1, cr), ln_b[l].reshape(1, cr), w_proj_mlstm[l].astype(BF16),
                      w_proj_rwkv[l].astype(BF16), w_out[l].astype(BF16), nt_out)

        i = l // 2
        if l % 2 == 0:
            assert need_ctx
            x_all = ffn_dense(x_mid, g_norm_ffn[l], modtab, w_ff_gate[i].astype(BF16), w_ff_up[i].astype(BF16),
                              w_ff_down[i].astype(BF16), w_ff_gate.shape[2] // 2)
        else:
            assert not need_ctx
            wr = jnp.concatenate([w_router[i], jnp.zeros((d, LANES - N_EXPERTS), F32)], axis=1)
            out = moe_final(x_mid, g_norm_ffn[l], modtab, wr, w_exp_gate[i].astype(BF16), w_exp_up[i].astype(BF16),
                            w_exp_down[i].astype(BF16), g_final, w_exp_gate.shape[3] // 2)
    return out
```

```python
import functools

import jax
import jax.numpy as jnp
import numpy as np
from jax import lax
from jax.experimental import pallas as pl
from jax.experimental.pallas import tpu as pltpu

F32 = jnp.float32
BF16 = jnp.bfloat16
HIGHEST = lax.Precision.HIGHEST

GRID_W = 64
MLSTM_HEADS = 8
MLSTM_DQK = 64
MLSTM_DV = 128
GATE_SOFT_CAP = 15.0
RWKV_HEADS = 16
RWKV_N = 64
DECAY_LORA = 64
AAA_LORA = 64
GATE_LORA = 128
GN_EPS = 64e-5
N_EXPERTS = 8
NORM_EPS = 1e-6

LANES = 128
MXU_DIM = 256
ROW_TILE = 256
PROJ_ROWS = 1024
VMEM_LIMIT = 56 * 1024 * 1024

SH1, SC1, GT1, SH2, SC2, GT2 = range(6)
MOD_ROWS = 8


def _cparams(sem):
    return pltpu.CompilerParams(dimension_semantics=sem, vmem_limit_bytes=VMEM_LIMIT)


def _sigmoid(x):
    return 1.0 / (1.0 + jnp.exp(-x))


def _block_diag_ones(block):
    r = lax.broadcasted_iota(jnp.int32, (MXU_DIM, MXU_DIM), 0)
    c = lax.broadcasted_iota(jnp.int32, (MXU_DIM, MXU_DIM), 1)
    sh = int(np.log2(block))
    return jnp.where((r >> sh) == (c >> sh), 1.0, 0.0).astype(BF16)


def _group_sum(x, bd):
    outs = []
    for c in range(x.shape[1] // MXU_DIM):
        xs = x[:, c * MXU_DIM:(c + 1) * MXU_DIM]
        hi = xs.astype(BF16)
        r1 = xs - hi.astype(F32)
        mid = r1.astype(BF16)
        lo = (r1 - mid.astype(F32)).astype(BF16)
        acc = jnp.dot(hi, bd, preferred_element_type=F32)
        acc += jnp.dot(mid, bd, preferred_element_type=F32)
        acc += jnp.dot(lo, bd, preferred_element_type=F32)
        outs.append(acc)
    return jnp.concatenate(outs, axis=1)


def _rms_mod(x, g, mod, shift_row, scale_row):
    ms = jnp.mean(x * x, axis=-1, keepdims=True)
    y = x * lax.rsqrt(ms + NORM_EPS) * g
    return y * (1.0 + mod[scale_row:scale_row + 1, :]) + mod[shift_row:shift_row + 1, :]


def _ada_kernel(s_ref, w_ref, b_ref, o_ref):
    s = s_ref[...]
    s = s * _sigmoid(s)
    o_ref[...] = jnp.dot(s, w_ref[...], precision=HIGHEST, preferred_element_type=F32) + b_ref[...]


def ada_mod(cond, w_ada, b_ada):
    depth, d, n = w_ada.shape
    tn = n // 4
    return pl.pallas_call(
        _ada_kernel,
        grid=(depth, n // tn),
        in_specs=[pl.BlockSpec((MOD_ROWS, d), lambda l, j: (0, 0)),
                  pl.BlockSpec((None, d, tn), lambda l, j: (l, 0, j)),
                  pl.BlockSpec((None, 1, tn), lambda l, j: (l, 0, j))],
        out_specs=pl.BlockSpec((None, MOD_ROWS, tn), lambda l, j: (l, 0, j)),
        out_shape=jax.ShapeDtypeStruct((depth, MOD_ROWS, n), F32),
        compiler_params=_cparams(("arbitrary", "arbitrary")),
        name="ada_mod",
    )(cond, w_ada, b_ada.reshape(depth, 1, n))


def _norm_mod_kernel(x_ref, g_ref, mod_ref, hb_ref, hf_ref):
    h = _rms_mod(x_ref[...], g_ref[...], mod_ref[...], SH1, SC1)
    hb_ref[...] = h.astype(BF16)
    hf_ref[...] = h


def norm_mod(x_all, g, modtab):
    b, tt, d = x_all.shape
    nt = tt // ROW_TILE
    ctx_tile = nt - 1
    spec = pl.BlockSpec((None, ROW_TILE, d), lambda bi, i: (bi, i, 0))
    return pl.pallas_call(
        _norm_mod_kernel,
        grid=(b, nt),
        in_specs=[spec,
                  pl.BlockSpec((1, d), lambda bi, i: (0, 0)),
                  pl.BlockSpec((None, None, MOD_ROWS, d), lambda bi, i: (bi, jnp.where(i == ctx_tile, 0, 1), 0, 0))],
        out_specs=[spec, spec],
        out_shape=[jax.ShapeDtypeStruct((b, tt, d), BF16), jax.ShapeDtypeStruct((b, tt, d), F32)],
        compiler_params=_cparams(("parallel", "parallel")),
        name="norm_mod",
    )(x_all, g.reshape(1, d), modtab)


def _mm_kernel(a_ref, w_ref, o_ref, *, precision):
    o_ref[...] = jnp.dot(a_ref[...], w_ref[...], precision=precision,
                         preferred_element_type=F32).astype(o_ref.dtype)


def matmul(a, w, tm, tn, out_dtype=F32, precision=None):
    m, k = a.shape
    n = w.shape[1]
    return pl.pallas_call(
        functools.partial(_mm_kernel, precision=precision),
        grid=(n // tn, m // tm),
        in_specs=[pl.BlockSpec((tm, k), lambda j, i: (i, 0)),
                  pl.BlockSpec((k, tn), lambda j, i: (0, j))],
        out_specs=pl.BlockSpec((tm, tn), lambda j, i: (i, j)),
        out_shape=jax.ShapeDtypeStruct((m, n), out_dtype),
        compiler_params=_cparams(("parallel", "parallel")),
        name="matmul",
    )(a, w)


def _mlstm_kernel(qf, kf, vf, gif, gff, qb, kb, vb, gib, gfb, bi_ref, bf_ref, of, ob, ct_ref, m_ref):
    @pl.when(pl.program_id(1) == 0)
    def _():
        ct_ref[...] = jnp.zeros_like(ct_ref)
        m_ref[...] = jnp.zeros_like(m_ref)

    gates = (_mlstm_gates(gif, gff, bi_ref, bf_ref, m_ref.at[0], False),
             _mlstm_gates(gib, gfb, bi_ref, bf_ref, m_ref.at[1], True))
    refs = ((qf, kf, vf, of), (qb, kb, vb, ob))
    for h in range(MLSTM_HEADS):
        for d in range(2):
            _mlstm_head(h, *refs[d], ct_ref.at[d], gates[d], bool(d))


def _running_max(x, rev):
    n = x.shape[0]
    rowi = lax.broadcasted_iota(jnp.int32, (n, 1), 0)
    sh = 1
    while sh < n:
        if rev:
            x = jnp.maximum(x, jnp.where(rowi < n - sh, pltpu.roll(x, n - sh, 0), -jnp.inf))
        else:
            x = jnp.maximum(x, jnp.where(rowi >= sh, pltpu.roll(x, sh, 0), -jnp.inf))
        sh *= 2
    return x


def _mlstm_gates(gi_ref, gf_ref, bi_ref, bf_ref, m_ref, rev):
    L = ROW_TILE

    def cap(a):
        return GATE_SOFT_CAP * jnp.tanh(a / GATE_SOFT_CAP)

    ic = cap(gi_ref[...] + bi_ref[...])
    fp = cap(gf_ref[...] + bf_ref[...])
    lf = jnp.minimum(fp, 0.0) - jnp.log(1.0 + jnp.exp(-jnp.abs(fp)))
    row = lax.broadcasted_iota(jnp.int32, (L, L), 0)
    col = lax.broadcasted_iota(jnp.int32, (L, L), 1)
    tri = (col >= row) if rev else (col <= row)
    b = jnp.dot(jnp.where(tri, 1.0, 0.0), lf, precision=HIGHEST, preferred_element_type=F32)
    src = ic - b
    m_prev = m_ref[0:1, :]
    mu = jnp.maximum(m_prev, _running_max(src, rev))
    w_inter = jnp.exp(m_prev - mu)
    e_den = jnp.exp(-(b + mu))
    last = 0 if rev else L - 1
    b_last = b[last:last + 1, :]
    w_src = b_last - b + ic
    m_new = jnp.maximum(b_last + m_prev, jnp.max(w_src, axis=0, keepdims=True))
    a_src = jnp.exp(w_src - m_new)
    a_old = jnp.exp(b_last + m_prev - m_new)
    m_ref[...] = jnp.broadcast_to(m_new, m_ref.shape)
    return src.T, mu, w_inter, e_den, a_src, a_old, tri


def _mlstm_head(h, q_ref, k_ref, v_ref, o_ref, ct_ref, gates, rev):
    L = ROW_TILE
    H, DQK, DV = MLSTM_HEADS, MLSTM_DQK, MLSTM_DV
    src_t, mu_all, w_inter_all, e_den_all, a_src_all, a_old_all, tri = gates
    lane0 = lax.broadcasted_iota(jnp.int32, (L, LANES), 1) == 0
    ones_blk = jnp.where(lane0, 1.0, 0.0).astype(BF16)
    scale = DQK ** -0.5
    ln = (H if rev else 0) + h
    col = lambda a: a[:, ln:ln + 1]
    w_intra = jnp.exp(jnp.where(tri, src_t[ln:ln + 1, :] - col(mu_all), -jnp.inf))
    w_inter = col(w_inter_all)
    qb = (q_ref[:, h * DQK:(h + 1) * DQK] * scale).astype(BF16)
    kf = k_ref[:, h * DQK:(h + 1) * DQK]
    kb = kf.astype(BF16)
    vaug = jnp.concatenate([v_ref[:, h * DV:(h + 1) * DV].astype(BF16), ones_blk], axis=1)
    sqk = lax.dot_general(qb, kb, (((1,), (1,)), ((), ())), preferred_element_type=F32)
    sw = (sqk * w_intra).astype(BF16)
    ct = ct_ref[h]
    qw = (q_ref[:, h * DQK:(h + 1) * DQK] * (scale * w_inter)).astype(BF16)
    num_aug = jnp.dot(jnp.concatenate([sw, qw], axis=1), jnp.concatenate([vaug, ct.astype(BF16)], axis=0),
                      preferred_element_type=F32)
    num = num_aug[:, :DV]
    den = num_aug[:, DV:DV + 1]
    o_ref[:, h * DV:(h + 1) * DV] = num / jnp.maximum(jnp.abs(den), col(e_den_all))
    ks = (kf * col(a_src_all)).astype(BF16)
    upd = lax.dot_general(ks, vaug, (((0,), (0,)), ((), ())), preferred_element_type=F32)
    ct_ref[h] = col(a_old_all) * ct + upd


def mlstm_scan(qkvo, gates, bias_i, bias_f):
    b, tt, _ = qkvo.shape
    nt = tt // ROW_TILE
    ctx_tile = nt - 1
    hq = MLSTM_HEADS * MLSTM_DQK
    hv = MLSTM_HEADS * MLSTM_DV
    order_f = lambda s: jnp.where(s == 0, ctx_tile, s - 1)
    order_b = lambda s: jnp.where(s == 0, ctx_tile, ctx_tile - s)

    def operands(order):
        return [pl.BlockSpec((None, ROW_TILE, hq), lambda bi, s: (bi, order(s), 0)),
                pl.BlockSpec((None, ROW_TILE, hq), lambda bi, s: (bi, order(s), 1)),
                pl.BlockSpec((None, ROW_TILE, hv), lambda bi, s: (bi, order(s), 1)),
                pl.BlockSpec((None, ROW_TILE, LANES), lambda bi, s: (bi, order(s), 0)),
                pl.BlockSpec((None, ROW_TILE, LANES), lambda bi, s: (bi, order(s), 1))]

    bias = pl.BlockSpec((1, LANES), lambda bi, s: (0, 0))
    return pl.pallas_call(
        _mlstm_kernel,
        grid=(b, nt),
        in_specs=operands(order_f) + operands(order_b) + [bias, bias],
        out_specs=[pl.BlockSpec((None, ROW_TILE, hv), lambda bi, s: (bi, order_f(s), 0)),
                   pl.BlockSpec((None, ROW_TILE, hv), lambda bi, s: (bi, order_b(s), 0))],
        out_shape=[jax.ShapeDtypeStruct((b, tt, hv), F32)] * 2,
        scratch_shapes=[pltpu.VMEM((2, MLSTM_HEADS, MLSTM_DQK, 2 * MLSTM_DV), F32),
                        pltpu.VMEM((2, 8, LANES), F32)],
        compiler_params=_cparams(("parallel", "arbitrary")),
        name="mlstm",
    )(*([qkvo] * 3 + [gates] * 2) * 2, bias_i, bias_f)


OP_KK, OP_R, OP_WF, OP_BF, OP_KF, OP_WB, OP_BB, OP_KB = range(8)
N_OPS = 8


def _rwkv_prep_kernel(p_ref, hp_ref, hn_ref, mu_ref, w0_ref, w2_ref, a0_ref, a2_ref, g2_ref, kk_ref, ka_ref,
                      rk_ref, ops_o, v_o, g_o, bonus_o, xs_ref):
    T = ROW_TILE
    i = pl.program_id(1)
    n_lat = pl.num_programs(1) - 1
    x = p_ref[...]
    W = x.shape[1]
    rowi = lax.broadcasted_iota(jnp.int32, (T, 1), 0)
    coli = lax.broadcasted_iota(jnp.int32, (1, W), 1)
    prev = pltpu.roll(x, 1, 0)
    nxt = pltpu.roll(x, T - 1, 0)

    @pl.when(i < n_lat)
    def _():
        gc = rowi & (GRID_W - 1)
        left = jnp.where(gc == 0, 0.0, prev)
        right = jnp.where(gc == GRID_W - 1, 0.0, nxt)
        hp = jnp.where(i == 0, 0.0, hp_ref[...])
        hn = jnp.where(i == n_lat - 1, 0.0, hn_ref[...])
        up = jnp.concatenate([hp, x[:T - GRID_W]], axis=0)
        down = jnp.concatenate([x[GRID_W:], hn], axis=0)
        q = W // 4
        xs_ref[...] = jnp.where(coli < q, left, jnp.where(coli < 2 * q, right, jnp.where(coli < 3 * q, up, down)))

    @pl.when(i == n_lat)
    def _():
        xs_ref[...] = jnp.where(coli < W // 2, jnp.where(rowi == 0, 0.0, prev), jnp.where(rowi == T - 1, 0.0, nxt))

    xm = x + (xs_ref[...] - x) * mu_ref[...]
    C = RWKV_HEADS * RWKV_N
    r = xm[:, 0:C]
    k = xm[:, C:2 * C]
    v = xm[:, 2 * C:3 * C]
    wd = jnp.tanh(xm[:, 3 * C:3 * C + 2 * DECAY_LORA]).astype(BF16)
    ad = xm[:, 3 * C + 2 * DECAY_LORA:3 * C + 2 * DECAY_LORA + 2 * AAA_LORA].astype(BF16)
    gd = _sigmoid(xm[:, 3 * C + 2 * DECAY_LORA + 2 * AAA_LORA:]).astype(BF16)
    bd = _block_diag_ones(RWKV_N)

    kk = k * kk_ref[...]
    nrm = jnp.sqrt(_group_sum(kk * kk, bd))
    kk = kk / jnp.maximum(nrm, 1e-12)
    u = w0_ref[...] + jnp.dot(wd, w2_ref[...], preferred_element_type=F32)
    decay = jnp.exp(-np.float32(np.exp(-0.5)) * _sigmoid(u))
    a = _sigmoid(a0_ref[...] + jnp.dot(ad, a2_ref[...], preferred_element_type=F32))
    ka = ka_ref[...]
    kd_f = k * (1.0 + (a[:, :C] - 1.0) * ka)
    kd_b = k * (1.0 + (a[:, C:] - 1.0) * ka)
    ops_o[OP_R] = r.T
    v_o[...] = v.T
    ops_o[OP_KK] = kk.T
    ops_o[OP_WF] = decay[:, :C].T
    ops_o[OP_WB] = decay[:, C:].T
    ops_o[OP_BF] = (kk * a[:, :C]).T
    ops_o[OP_BB] = (kk * a[:, C:]).T
    ops_o[OP_KF] = kd_f.T
    ops_o[OP_KB] = kd_b.T
    g_o[...] = jnp.dot(gd, g2_ref[...], preferred_element_type=F32)
    bonus_o[...] = _group_sum(r * (kd_f + kd_b) * rk_ref[...], bd) * v


def rwkv_prepare(p, mu, w0cat, w2cat, a0cat, a2cat, g2, k_k, k_a, r_k):
    b, tt, w = p.shape
    nt = tt // ROW_TILE
    n_lat = nt - 1
    hpt = ROW_TILE // GRID_W
    c = RWKV_HEADS * RWKV_N
    n_halo = tt // GRID_W
    const = lambda shape: pl.BlockSpec(shape, lambda bi, i: (0,) * len(shape))
    out_spec = pl.BlockSpec((None, ROW_TILE, c), lambda bi, i: (bi, i, 0))
    return pl.pallas_call(
        _rwkv_prep_kernel,
        grid=(b, nt),
        in_specs=[pl.BlockSpec((None, ROW_TILE, w), lambda bi, i: (bi, i, 0)),
                  pl.BlockSpec((None, GRID_W, w), lambda bi, i: (bi, jnp.maximum(i * hpt - 1, 0), 0)),
                  pl.BlockSpec((None, GRID_W, w), lambda bi, i: (bi, jnp.minimum(i * hpt + hpt, n_halo - 1), 0)),
                  const((1, w)), const((1, 2 * c)), const((2 * DECAY_LORA, 2 * c)), const((1, 2 * c)),
                  const((2 * AAA_LORA, 2 * c)), const((GATE_LORA, c)), const((1, c)), const((1, c)), const((1, c))],
        out_specs=[pl.BlockSpec((N_OPS, None, c, ROW_TILE), lambda bi, i: (0, bi, 0, i)),
                   pl.BlockSpec((None, c, ROW_TILE), lambda bi, i: (bi, 0, i)), out_spec, out_spec],
        out_shape=[jax.ShapeDtypeStruct((N_OPS, b, c, tt), F32), jax.ShapeDtypeStruct((b, c, tt), F32),
                   jax.ShapeDtypeStruct((b, tt, c), F32), jax.ShapeDtypeStruct((b, tt, c), F32)],
        scratch_shapes=[pltpu.VMEM((ROW_TILE, w), F32)],
        compiler_params=_cparams(("parallel", "parallel")),
        name="rwkv_prepare",
    )(p, p, p, mu, w0cat, w2cat, a0cat, a2cat, g2, k_k, k_a, r_k)


RELAYOUT_ROWS = 128
V_HALF = RWKV_N // 2


def _to_scan_k_kernel(x_ref, o_ref):
    nbh = x_ref.shape[0] // RWKV_N
    for k in range(RWKV_N):
        g = x_ref[pl.ds(k, nbh, stride=RWKV_N), :]
        o_ref[k] = jnp.concatenate([g, g], axis=0).T


def to_scan_k(ops_t):
    n_ops, b, c, tt = ops_t.shape
    return pl.pallas_call(
        _to_scan_k_kernel,
        grid=(n_ops, tt // RELAYOUT_ROWS),
        in_specs=[pl.BlockSpec((None, b * c, RELAYOUT_ROWS), lambda o, i: (o, 0, i))],
        out_specs=pl.BlockSpec((None, RWKV_N, RELAYOUT_ROWS, LANES), lambda o, i: (o, 0, i, 0)),
        out_shape=jax.ShapeDtypeStruct((n_ops, RWKV_N, tt, LANES), F32),
        compiler_params=_cparams(("parallel", "parallel")),
        name="to_scan_k",
    )(ops_t.reshape(n_ops, b * c, tt))


def _to_scan_v_kernel(x_ref, o_ref):
    nbh = x_ref.shape[0] // RWKV_N
    for v in range(V_HALF):
        m = jnp.concatenate([x_ref[pl.ds(v, nbh, stride=RWKV_N), :],
                             x_ref[pl.ds(V_HALF + v, nbh, stride=RWKV_N), :]], axis=0)
        o_ref[:, v, :] = m.T


def to_scan_v(x_t):
    b, c, tt = x_t.shape
    return pl.pallas_call(
        _to_scan_v_kernel,
        grid=(tt // RELAYOUT_ROWS,),
        in_specs=[pl.BlockSpec((b * c, RELAYOUT_ROWS), lambda i: (0, i))],
        out_specs=pl.BlockSpec((RELAYOUT_ROWS, V_HALF, LANES), lambda i: (i, 0, 0)),
        out_shape=jax.ShapeDtypeStruct((tt, V_HALF, LANES), F32),
        compiler_params=_cparams(("parallel",)),
        name="to_scan_v",
    )(x_t.reshape(b * c, tt))


def _from_scan_kernel(yf_ref, yb_ref, o_ref, zs_ref):
    nbh = zs_ref.shape[0] // RWKV_N
    for v in range(V_HALF):
        t = (yf_ref[:, v, :] + yb_ref[:, v, :]).T
        zs_ref[pl.ds(v, nbh, stride=RWKV_N), :] = t[:nbh]
        zs_ref[pl.ds(V_HALF + v, nbh, stride=RWKV_N), :] = t[nbh:]
    for b in range(o_ref.shape[0]):
        for cb in range(o_ref.shape[2] // LANES):
            r0 = (b * (o_ref.shape[2] // LANES) + cb) * LANES
            o_ref[b, :, cb * LANES:(cb + 1) * LANES] = zs_ref[r0:r0 + LANES, :].T


def from_scan(yf, yb, b):
    tt = yf.shape[0]
    c = RWKV_HEADS * RWKV_N
    spec = pl.BlockSpec((RELAYOUT_ROWS, V_HALF, LANES), lambda i: (i, 0, 0))
    return pl.pallas_call(
        _from_scan_kernel,
        grid=(tt // RELAYOUT_ROWS,),
        in_specs=[spec, spec],
        out_specs=pl.BlockSpec((b, RELAYOUT_ROWS, c), lambda i: (0, i, 0)),
        out_shape=jax.ShapeDtypeStruct((b, tt, c), F32),
        scratch_shapes=[pltpu.VMEM((b * c, RELAYOUT_ROWS), F32)],
        compiler_params=_cparams(("parallel",)),
        name="from_scan",
    )(yf, yb)


SCAN_ROWS = 32


def _rwkv_scan_kernel(kk_f, kkn_f, r_f, w_f, b_f, k_f, kk_b, kkn_b, r_b, w_b, b_b, k_b, v_f, v_b, y_f, y_b,
                      s_ref, sa_ref, kkx_f, kkx_b):
    N, TB, PAD = RWKV_N, SCAN_ROWS, 8

    @pl.when(pl.program_id(0) == 0)
    def _():
        s_ref[...] = jnp.zeros_like(s_ref)
        sa_ref[...] = jnp.zeros_like(sa_ref)

    kkx_f[:, 0:TB, :] = kk_f[...]
    kkx_f[:, TB:TB + PAD, :] = kkn_f[:, 0:PAD, :]
    kkx_b[:, PAD:PAD + TB, :] = kk_b[...]
    kkx_b[:, 0:PAD, :] = kkn_b[:, TB - PAD:TB, :]

    dirs = ((kkx_f, r_f, w_f, b_f, k_f, v_f, y_f), (kkx_b, r_b, w_b, b_b, k_b, v_b, y_b))

    def step(j, carry):
        for d, (kkx_r, r_r, w_r, b_r, k_r, v_r, y_r) in enumerate(dirs):
            row, nxt = (j, j + 1) if d == 0 else (TB - 1 - j, PAD + TB - 2 - j)
            op = lambda ref, kx, rw: ref[kx, pl.ds(rw, 1), :]
            sa = sa_ref[d]
            vt = v_r[row]
            acc_sa = [jnp.zeros((V_HALF, LANES), F32), jnp.zeros((V_HALF, LANES), F32)]
            acc_y = [jnp.zeros((V_HALF, LANES), F32), jnp.zeros((V_HALF, LANES), F32)]
            for kx in range(N):
                sk = s_ref[d, kx] * op(w_r, kx, row) - sa * op(b_r, kx, row) + vt * op(k_r, kx, row)
                s_ref[d, kx] = sk
                acc_sa[kx % 2] = acc_sa[kx % 2] + sk * op(kkx_r, kx, nxt)
                acc_y[kx % 2] = acc_y[kx % 2] + sk * op(r_r, kx, row)
            sa_ref[d] = acc_sa[0] + acc_sa[1]
            y_r[row] = acc_y[0] + acc_y[1]
        return carry

    lax.fori_loop(0, SCAN_ROWS, step, 0)


def rwkv_scan(ops_s, v_s, seq):
    _, n, tt, lanes = ops_s.shape
    n_lat = seq // SCAN_ROWS
    n_blk = tt // SCAN_ROWS
    n_ctx = n_blk - n_lat
    blk_f = lambda s: jnp.where(s < n_ctx, n_lat + s, s - n_ctx)
    blk_b = lambda s: n_blk - 1 - s
    kspec = lambda o, blk: pl.BlockSpec((None, n, SCAN_ROWS, lanes), lambda s: (o, 0, blk(s), 0))
    vspec = lambda blk: pl.BlockSpec((SCAN_ROWS, V_HALF, lanes), lambda s: (blk(s), 0, 0))
    nxt = lambda blk: (lambda s: blk(jnp.minimum(s + 1, n_blk - 1)))
    fwd = [kspec(OP_KK, blk_f), kspec(OP_KK, nxt(blk_f))] + [kspec(o, blk_f) for o in (OP_R, OP_WF, OP_BF, OP_KF)]
    bwd = [kspec(OP_KK, blk_b), kspec(OP_KK, nxt(blk_b))] + [kspec(o, blk_b) for o in (OP_R, OP_WB, OP_BB, OP_KB)]
    return pl.pallas_call(
        _rwkv_scan_kernel,
        grid=(n_blk,),
        in_specs=fwd + bwd + [vspec(blk_f), vspec(blk_b)],
        out_specs=[vspec(blk_f), vspec(blk_b)],
        out_shape=[jax.ShapeDtypeStruct((tt, V_HALF, lanes), F32)] * 2,
        scratch_shapes=[pltpu.VMEM((2, n, V_HALF, lanes), F32), pltpu.VMEM((2, V_HALF, lanes), F32),
                        pltpu.VMEM((n, SCAN_ROWS + 8, lanes), F32), pltpu.VMEM((n, SCAN_ROWS + 8, lanes), F32)],
        compiler_params=_cparams(("arbitrary",)),
        name="rwkv_scan",
    )(*([ops_s] * 12), v_s, v_s)


def _merge_kernel(hf_ref, hb_ref, o_ref, y_ref, bonus_ref, g_ref, mg_ref, x_ref, mod_ref, gm_ref, lnw_ref,
                  lnb_ref, wpm_ref, wpr_ref, wout_ref, out_ref):
    c = RWKV_HEADS * RWKV_N
    hm = hf_ref[...] + hb_ref[...]
    ms = _group_sum(hm * hm, _block_diag_ones(MLSTM_DV)) * (1.0 / MLSTM_DV)
    hmn = hm * lax.rsqrt(ms + NORM_EPS) * gm_ref[...] * _sigmoid(o_ref[...])
    ym = jnp.dot(hmn.astype(BF16), wpm_ref[...], preferred_element_type=F32)
    bd = _block_diag_ones(RWKV_N)
    y = y_ref[...]
    yc = y - _group_sum(y, bd) * (1.0 / RWKV_N)
    var = _group_sum(yc * yc, bd) * (1.0 / RWKV_N)
    yn = yc * lax.rsqrt(var + GN_EPS) * lnw_ref[...] + lnb_ref[...]
    yr_in = ((yn + bonus_ref[...]) * g_ref[...]).astype(BF16)
    yr = jnp.dot(yr_in, wpr_ref[...], preferred_element_type=F32)
    mg = mg_ref[...]
    z = _sigmoid(mg[:, :c]) * ym + _sigmoid(mg[:, c:]) * yr
    yy = jnp.dot(z.astype(BF16), wout_ref[...], preferred_element_type=F32)
    out_ref[...] = x_ref[...] + mod_ref[GT1:GT1 + 1, :] * yy


def merge(hf, hb, qkvo, y, bonus, g, mg, x_all, modtab, gm, lnw, lnb, wpm, wpr, wout, nt_out):
    b, tt, d = x_all.shape
    ctx_tile = tt // ROW_TILE - 1
    row = lambda width, colblk=0: pl.BlockSpec((None, ROW_TILE, width), lambda bi, i: (bi, i, colblk))
    const = lambda shape: pl.BlockSpec(shape, lambda bi, i: (0,) * len(shape))
    return pl.pallas_call(
        _merge_kernel,
        grid=(b, nt_out),
        in_specs=[row(d), row(d), row(d, 2), row(d), row(d), row(d), row(2 * d), row(d),
                  pl.BlockSpec((None, None, MOD_ROWS, d), lambda bi, i: (bi, jnp.where(i == ctx_tile, 0, 1), 0, 0)),
                  const((1, d)), const((1, d)), const((1, d)), const((d, d)), const((d, d)), const((d, d))],
        out_specs=row(d),
        out_shape=jax.ShapeDtypeStruct((b, nt_out * ROW_TILE, d), F32),
        compiler_params=_cparams(("parallel", "parallel")),
        name="merge",
    )(hf, hb, qkvo, y, bonus, g, mg, x_all, modtab, gm, lnw, lnb, wpm, wpr, wout)


def _ffn_kernel(x_ref, g_ref, mod_ref, wg_ref, wu_ref, wd_ref, o_ref, h_scr, acc_scr):
    j = pl.program_id(2)

    @pl.when(j == 0)
    def _():
        h_scr[...] = _rms_mod(x_ref[...], g_ref[...], mod_ref[...], SH2, SC2).astype(BF16)
        acc_scr[...] = jnp.zeros_like(acc_scr)

    h = h_scr[...]
    a = jnp.dot(h, wg_ref[...], preferred_element_type=F32)
    u = jnp.dot(h, wu_ref[...], preferred_element_type=F32)
    t = (a * _sigmoid(a) * u).astype(BF16)
    acc_scr[...] += jnp.dot(t, wd_ref[...], preferred_element_type=F32)

    @pl.when(j == pl.num_programs(2) - 1)
    def _():
        o_ref[...] = x_ref[...] + mod_ref[GT2:GT2 + 1, :] * acc_scr[...]


def ffn_dense(x_all, g, modtab, wg, wu, wd, tf):
    b, tt, d = x_all.shape
    nt = tt // ROW_TILE
    ctx_tile = nt - 1
    f = wg.shape[1]
    row = pl.BlockSpec((None, ROW_TILE, d), lambda bi, i, j: (bi, i, 0))
    return pl.pallas_call(
        _ffn_kernel,
        grid=(b, nt, f // tf),
        in_specs=[row,
                  pl.BlockSpec((1, d), lambda bi, i, j: (0, 0)),
                  pl.BlockSpec((None, None, MOD_ROWS, d),
                               lambda bi, i, j: (bi, jnp.where(i == ctx_tile, 0, 1), 0, 0)),
                  pl.BlockSpec((d, tf), lambda bi, i, j: (0, j)),
                  pl.BlockSpec((d, tf), lambda bi, i, j: (0, j)),
                  pl.BlockSpec((tf, d), lambda bi, i, j: (j, 0))],
        out_specs=row,
        out_shape=jax.ShapeDtypeStruct((b, tt, d), F32),
        scratch_shapes=[pltpu.VMEM((ROW_TILE, d), BF16), pltpu.VMEM((ROW_TILE, d), F32)],
        compiler_params=_cparams(("parallel", "parallel", "arbitrary")),
        name="ffn_dense",
    )(x_all, g.reshape(1, d), modtab, wg, wu, wd)


MOE_ROWS = 512


TOP_K = 2
MOE_TOKENS = 256
SEL_E1, SEL_E2, SEL_P1, SEL_P2 = range(4)


def _moe_route_kernel(x_ref, g_ref, mod_ref, wr_ref, h_o, sel_o):
    h = _rms_mod(x_ref[...], g_ref[...], mod_ref[...], SH2, SC2)
    h_o[...] = h
    logits = jnp.dot(h, wr_ref[...], precision=HIGHEST, preferred_element_type=F32)
    lane = lax.broadcasted_iota(jnp.int32, logits.shape, 1)
    logits = jnp.where(lane < N_EXPERTS, logits, -jnp.inf)
    m1 = jnp.max(logits, axis=-1, keepdims=True)
    i1 = jnp.min(jnp.where(logits == m1, lane, LANES), axis=-1, keepdims=True)
    rest = jnp.where(lane == i1, -jnp.inf, logits)
    m2 = jnp.max(rest, axis=-1, keepdims=True)
    i2 = jnp.min(jnp.where(rest == m2, lane, LANES), axis=-1, keepdims=True)
    e21 = jnp.exp(m2 - m1)
    p1 = 1.0 / (1.0 + e21)
    sel_o[...] = jnp.where(lane == SEL_E1, i1.astype(F32),
                           jnp.where(lane == SEL_E2, i2.astype(F32),
                                     jnp.where(lane == SEL_P1, p1, jnp.where(lane == SEL_P2, e21 * p1, 0.0))))


def moe_route(x_lat, g, modtab, w_router):
    b, s, d = x_lat.shape
    nt = s // MOE_ROWS
    return pl.pallas_call(
        _moe_route_kernel,
        grid=(b, nt),
        in_specs=[pl.BlockSpec((None, MOE_ROWS, d), lambda bi, i: (bi, i, 0)),
                  pl.BlockSpec((1, d), lambda bi, i: (0, 0)),
                  pl.BlockSpec((None, None, MOD_ROWS, d), lambda bi, i: (bi, 1, 0, 0)),
                  pl.BlockSpec((d, LANES), lambda bi, i: (0, 0))],
        out_specs=[pl.BlockSpec((MOE_ROWS, d), lambda bi, i: (bi * nt + i, 0)),
                   pl.BlockSpec((MOE_ROWS, LANES), lambda bi, i: (bi * nt + i, 0))],
        out_shape=[jax.ShapeDtypeStruct((b * s, d), F32), jax.ShapeDtypeStruct((b * s, LANES), F32)],
        compiler_params=_cparams(("parallel", "parallel")),
        name="moe_route",
    )(x_lat, g.reshape(1, d), modtab, w_router)


def _row_copy(src, src_row, dst, dst_row, sem):
    return pltpu.make_async_copy(src.at[pl.ds(src_row, 1)], dst.at[pl.ds(dst_row, 1)], sem)


ORDER_CHUNK = 1024


def _moe_dispatch_kernel(s0_ref, ord_lo, ord_hi, h_hbm, xg_ref, sem, *, n_choices):
    s0 = s0_ref[pl.program_id(0)]
    base = s0 - (s0 & (ORDER_CHUNK - 1))

    def start(r, c):
        j = jnp.minimum(s0 + r, n_choices - 1) - base
        choice = jnp.where(j < ORDER_CHUNK, ord_lo[jnp.minimum(j, ORDER_CHUNK - 1)],
                           ord_hi[jnp.maximum(j - ORDER_CHUNK, 0)])
        _row_copy(h_hbm, lax.shift_right_logical(choice, 1), xg_ref, r, sem).start()
        return c

    def wait(r, c):
        _row_copy(h_hbm, 0, xg_ref, 0, sem).wait()
        return c

    lax.fori_loop(0, MOE_ROWS, start, 0, unroll=8)
    lax.fori_loop(0, MOE_ROWS, wait, 0, unroll=8)


def moe_dispatch(s0, order, h, rows):
    d = h.shape[1]
    n_chunks = order.shape[0] // ORDER_CHUNK
    return pl.pallas_call(
        functools.partial(_moe_dispatch_kernel, n_choices=order.shape[0]),
        grid_spec=pltpu.PrefetchScalarGridSpec(
            num_scalar_prefetch=1,
            grid=(rows // MOE_ROWS,),
            in_specs=[pl.BlockSpec((ORDER_CHUNK,), lambda i, s: (s[i] // ORDER_CHUNK,), memory_space=pltpu.SMEM),
                      pl.BlockSpec((ORDER_CHUNK,), lambda i, s: (jnp.minimum(s[i] // ORDER_CHUNK + 1, n_chunks - 1),),
                                   memory_space=pltpu.SMEM),
                      pl.BlockSpec(memory_space=pl.ANY)],
            out_specs=pl.BlockSpec((MOE_ROWS, d), lambda i, s: (i, 0)),
            scratch_shapes=[pltpu.SemaphoreType.DMA(())]),
        out_shape=jax.ShapeDtypeStruct((rows, d), F32),
        compiler_params=_cparams(("arbitrary",)),
        name="moe_dispatch",
    )(s0, order, order, h)


def _moe_experts_kernel(te_ref, nu_ref, xg_ref, wg_ref, wu_ref, wd_ref, y_ref, h_scr, acc_scr):
    del te_ref
    i = pl.program_id(0)
    j = pl.program_id(1)
    used = i < nu_ref[0]
    last = j == pl.num_programs(1) - 1

    @pl.when(jnp.logical_and(used, j == 0))
    def _():
        h_scr[...] = xg_ref[...].astype(BF16)
        acc_scr[...] = jnp.zeros_like(acc_scr)

    @pl.when(used)
    def _():
        h = h_scr[...]
        a = jnp.dot(h, wg_ref[...], preferred_element_type=F32)
        u = jnp.dot(h, wu_ref[...], preferred_element_type=F32)
        t = (a * _sigmoid(a) * u).astype(BF16)
        acc_scr[...] += jnp.dot(t, wd_ref[...], preferred_element_type=F32)

    @pl.when(jnp.logical_and(used, last))
    def _():
        y_ref[...] = acc_scr[...]

    @pl.when(jnp.logical_and(jnp.logical_not(used), last))
    def _():
        y_ref[...] = jnp.zeros_like(y_ref)


def moe_experts(tile_expert, n_used, xg, wg, wu, wd, tf):
    rows, d = xg.shape
    f = wg.shape[2]
    return pl.pallas_call(
        _moe_experts_kernel,
        grid_spec=pltpu.PrefetchScalarGridSpec(
            num_scalar_prefetch=2,
            grid=(rows // MOE_ROWS, f // tf),
            in_specs=[pl.BlockSpec((MOE_ROWS, d), lambda i, j, te, nu: (i, 0)),
                      pl.BlockSpec((None, d, tf), lambda i, j, te, nu: (te[i], 0, j)),
                      pl.BlockSpec((None, d, tf), lambda i, j, te, nu: (te[i], 0, j)),
                      pl.BlockSpec((None, tf, d), lambda i, j, te, nu: (te[i], j, 0))],
            out_specs=pl.BlockSpec((MOE_ROWS, d), lambda i, j, te, nu: (i, 0)),
            scratch_shapes=[pltpu.VMEM((MOE_ROWS, d), BF16), pltpu.VMEM((MOE_ROWS, d), F32)]),
        out_shape=jax.ShapeDtypeStruct((rows, d), F32),
        compiler_params=_cparams(("arbitrary", "arbitrary")),
        name="moe_experts",
    )(tile_expert, n_used, xg, wg, wu, wd)


def _moe_combine_kernel(pos_ref, sel_ref, x_ref, mod_ref, gfin_ref, y_hbm, o_ref, ybuf, sem):
    n = pos_ref.shape[-1]

    def start(a, c):
        _row_copy(y_hbm, pos_ref[0, a], ybuf.at[a & 1], lax.shift_right_logical(a, 1), sem).start()
        return c

    def wait(a, c):
        _row_copy(y_hbm, 0, ybuf.at[0], 0, sem).wait()
        return c

    lax.fori_loop(0, n, start, 0, unroll=8)
    lax.fori_loop(0, n, wait, 0, unroll=8)
    sel = sel_ref[...]
    yy = sel[:, SEL_P1:SEL_P1 + 1] * ybuf[0] + sel[:, SEL_P2:SEL_P2 + 1] * ybuf[1]
    xo = x_ref[...] + mod_ref[GT2:GT2 + 1, :] * yy
    ms = jnp.mean(xo * xo, axis=-1, keepdims=True)
    o_ref[...] = xo * lax.rsqrt(ms + NORM_EPS) * gfin_ref[...]


def moe_combine(pos, sel, x_lat, modtab, g_final, y):
    b, s, d = x_lat.shape
    nt = s // MOE_TOKENS
    n = TOP_K * MOE_TOKENS
    row = pl.BlockSpec((None, MOE_TOKENS, d), lambda bi, i: (bi, i, 0))
    return pl.pallas_call(
        _moe_combine_kernel,
        grid=(b, nt),
        in_specs=[pl.BlockSpec((None, 1, n), lambda bi, i: (bi * nt + i, 0, 0), memory_space=pltpu.SMEM),
                  pl.BlockSpec((MOE_TOKENS, LANES), lambda bi, i: (bi * nt + i, 0)),
                  row,
                  pl.BlockSpec((None, None, MOD_ROWS, d), lambda bi, i: (bi, 1, 0, 0)),
                  pl.BlockSpec((1, d), lambda bi, i: (0, 0)),
                  pl.BlockSpec(memory_space=pl.ANY)],
        out_specs=row,
        out_shape=jax.ShapeDtypeStruct((b, s, d), F32),
        scratch_shapes=[pltpu.VMEM((TOP_K, MOE_TOKENS, d), F32), pltpu.SemaphoreType.DMA(())],
        compiler_params=_cparams(("arbitrary", "arbitrary")),
        name="moe_combine",
    )(pos.reshape(b * nt, 1, n), sel, x_lat, modtab, g_final.reshape(1, d), y)


def moe_final(x_lat, g, modtab, w_router, wg, wu, wd, g_final, tf):
    b, s, d = x_lat.shape
    m = b * s
    ne = wg.shape[0]
    h, sel = moe_route(x_lat, g, modtab, w_router)
    e = sel[:, SEL_E1:SEL_E2 + 1].astype(jnp.int32).reshape(m * TOP_K)
    onehot = (e[:, None] == jnp.arange(ne, dtype=jnp.int32)[None, :]).astype(jnp.int32)
    running = jnp.cumsum(onehot, axis=0)
    counts = running[-1]
    padded = (counts + MOE_ROWS - 1) // MOE_ROWS * MOE_ROWS
    group_end = jnp.cumsum(padded)
    group_start = group_end - padded
    pos = jnp.sum(onehot * (group_start[None, :] + running - 1), axis=1).astype(jnp.int32)
    n_tiles = m * TOP_K // MOE_ROWS + ne
    tile_expert = jnp.minimum(
        jnp.sum(jnp.arange(n_tiles, dtype=jnp.int32)[:, None] >= (group_end // MOE_ROWS)[None, :], axis=1),
        ne - 1).astype(jnp.int32)
    n_used = (group_end[-1:] // MOE_ROWS).astype(jnp.int32)
    order = jnp.argsort(e, stable=True).astype(jnp.int32)
    pad_before = group_start - (jnp.cumsum(counts) - counts)
    first_row = jnp.arange(n_tiles, dtype=jnp.int32) * MOE_ROWS
    s0 = jnp.clip(first_row - pad_before[tile_expert[first_row // MOE_ROWS]], 0,
                  m * TOP_K - 1).astype(jnp.int32)
    xg = moe_dispatch(s0, order, h, n_tiles * MOE_ROWS)
    y = moe_experts(tile_expert, n_used, xg, wg, wu, wd, tf)
    return moe_combine(pos, sel, x_lat, modtab, g_final, y)


def kernel(x, c, ctx, c_ctx, w_ada, b_ada, g_norm_mix, g_norm_ffn, w_in, b_mlstm_gate, g_mlstm_norm, mu_shift, w0, w2, a0, a2, g2, k_k, k_a, r_k, ln_w, ln_b, w_proj_mlstm, w_proj_rwkv, w_out, w_ff_gate, w_ff_up, w_ff_down, w_router, w_exp_gate, w_exp_up, w_exp_down, g_final):
    b, seq, d = x.shape
    ctx_len = ctx.shape[1]
    depth = w_ada.shape[0]
    assert ctx_len == ROW_TILE and seq % ROW_TILE == 0 and seq % MOE_ROWS == 0 and 2 * b * RWKV_HEADS == LANES
    assert (b * (seq + ctx_len)) % PROJ_ROWS == 0
    tt = seq + ctx_len
    m_all = b * tt
    H = MLSTM_HEADS
    cr = RWKV_HEADS * RWKV_N
    hq, hv = H * MLSTM_DQK, H * MLSTM_DV

    x_all = jnp.concatenate([x, ctx], axis=1)
    cond = jnp.concatenate([c, c_ctx[None, :], jnp.zeros((MOD_ROWS - b - 1, d), F32)], axis=0)
    mod_all = ada_mod(cond, w_ada, b_ada).reshape(depth, MOD_ROWS, 6, d)

    o_gate = 2 * hq + 2 * hv
    o_rwkv = o_gate + 4 * H
    rwkv_w = 3 * cr + 2 * DECAY_LORA + 2 * AAA_LORA + GATE_LORA
    o_mg = o_rwkv + rwkv_w

    out = None
    for l in range(depth):
        need_ctx = l < depth - 1
        pad = jnp.zeros((b, MOD_ROWS - 6, d), F32)
        mod_lat = jnp.concatenate([mod_all[l, :b], pad], axis=1)
        mod_ctx = jnp.broadcast_to(jnp.concatenate([mod_all[l, b], pad[0]], axis=0), (b, MOD_ROWS, d))
        modtab = jnp.stack([mod_ctx, mod_lat], axis=1)

        wl = w_in[l]
        w_qkvo = wl[:, :o_gate].astype(BF16)
        wgt = wl[:, o_gate:o_rwkv].reshape(d, 4, H)
        zpad = jnp.zeros((d, LANES - 2 * H), F32)
        w_gates = jnp.concatenate([wgt[:, 0], wgt[:, 2], zpad, wgt[:, 1], wgt[:, 3], zpad], axis=1)
        bg = b_mlstm_gate[l]
        lpad = jnp.zeros((LANES - 2 * H,), F32)
        bias_i = jnp.concatenate([bg[0], bg[2], lpad]).reshape(1, LANES)
        bias_f = jnp.concatenate([bg[1], bg[3], lpad]).reshape(1, LANES)
        w_rwkv = wl[:, o_rwkv:o_mg].astype(BF16)
        w_mg = wl[:, o_mg:].astype(BF16)

        hb16, hf32 = norm_mod(x_all, g_norm_mix[l], modtab)
        hb16 = hb16.reshape(m_all, d)
        qkvo = matmul(hb16, w_qkvo, PROJ_ROWS, 1024).reshape(b, tt, o_gate)
        gates = matmul(hf32.reshape(m_all, d), w_gates, 512, 2 * LANES, precision=HIGHEST).reshape(b, tt, 2 * LANES)
        prw = matmul(hb16, w_rwkv, PROJ_ROWS, rwkv_w // 3).reshape(b, tt, rwkv_w)
        mg = matmul(hb16, w_mg, PROJ_ROWS, 1024).reshape(b, tt, 2 * d)

        h_f, h_b = mlstm_scan(qkvo, gates, bias_i, bias_f)

        zl = jnp.zeros((DECAY_LORA, cr), F32)
        w2cat = jnp.concatenate([jnp.concatenate([w2[l, 0], zl], axis=1),
                                 jnp.concatenate([zl, w2[l, 1]], axis=1)], axis=0).astype(BF16)
        a2cat = jnp.concatenate([jnp.concatenate([a2[l, 0], zl], axis=1),
                                 jnp.concatenate([zl, a2[l, 1]], axis=1)], axis=0).astype(BF16)
        ops_, v_, g_, bonus_ = rwkv_prepare(
            prw, mu_shift[l].reshape(1, rwkv_w), w0[l].reshape(1, 2 * cr), w2cat, a0[l].reshape(1, 2 * cr), a2cat,
            g2[l].astype(BF16), k_k[l].reshape(1, cr), k_a[l].reshape(1, cr), r_k[l].reshape(1, cr))
        y_f, y_b = rwkv_scan(to_scan_k(ops_), to_scan_v(v_), seq)
        y_ = from_scan(y_f, y_b, b)

        nt_out = tt // ROW_TILE if need_ctx else seq // ROW_TILE
        x_mid = merge(h_f, h_b, qkvo, y_, bonus_, g_, mg, x_all, modtab, g_mlstm_norm[l].reshape(1, hv),
                      ln_w[l].reshape(1, cr), ln_b[l].reshape(1, cr), w_proj_mlstm[l].astype(BF16),
                      w_proj_rwkv[l].astype(BF16), w_out[l].astype(BF16), nt_out)

        i = l // 2
        if l % 2 == 0:
            assert need_ctx
            x_all = ffn_dense(x_mid, g_norm_ffn[l], modtab, w_ff_gate[i].astype(BF16), w_ff_up[i].astype(BF16),
                              w_ff_down[i].astype(BF16), w_ff_gate.shape[2])
        else:
            assert not need_ctx
            wr = jnp.concatenate([w_router[i], jnp.zeros((d, LANES - N_EXPERTS), F32)], axis=1)
            out = moe_final(x_mid, g_norm_ffn[l], modtab, wr, w_exp_gate[i].astype(BF16), w_exp_up[i].astype(BF16),
                            w_exp_down[i].astype(BF16), g_final, w_exp_gate.shape[3] // 2)
    return out
```
